```python
import jax, jax.numpy as jnp
from jax import lax
import numpy as np

D_MODEL = 1024
BATCH = 2
SEQ = 8192
DEPTH = 1
DEC_BATCH = 8
DEC_SEQ = 32
PAST_LEN = 4096

CHUNK = 64
D_MIX = D_MODEL
D_CONV = D_MIX // 2
D_RET = D_MIX - D_CONV
N_RET_HEADS = 4
HEAD_DIM = D_RET // N_RET_HEADS
CONV_WIDTH = 31
D_FF = 2816
ROPE_THETA = 10000.0
EPS = 1e-6
D_IN = 2 * D_CONV + 4 * D_RET

kernel_name = "hybrid_conformer_retention_stream_step"


def rmsnorm(x, g):
    xf = x.astype(jnp.float32)
    y = xf * lax.rsqrt(jnp.mean(xf * xf, axis=-1, keepdims=True) + EPS)
    return (y * g.astype(jnp.float32)).astype(x.dtype)


def layernorm(x, g, b):
    xf = x.astype(jnp.float32)
    mu = jnp.mean(xf, axis=-1, keepdims=True)
    var = jnp.mean(jnp.square(xf - mu), axis=-1, keepdims=True)
    y = (xf - mu) * lax.rsqrt(var + EPS)
    return (y * g.astype(jnp.float32) + b.astype(jnp.float32)).astype(x.dtype)


def swiglu(x, w_gate, w_up, w_down):
    return (jax.nn.silu(x @ w_gate) * (x @ w_up)) @ w_down


def rope(x, pos):
    half = HEAD_DIM // 2
    inv = ROPE_THETA ** (-jnp.arange(half, dtype=jnp.float32) / half)
    ang = pos.astype(jnp.float32)[:, None] * inv[None, :]
    cos, sin = jnp.cos(ang), jnp.sin(ang)
    x1 = x[..., :half].astype(jnp.float32)
    x2 = x[..., half:].astype(jnp.float32)
    out = jnp.concatenate([x1 * cos - x2 * sin, x2 * cos + x1 * sin], axis=-1)
    return out.astype(x.dtype)


def retention(q, k, v, R0):
    B, H, T, _ = q.shape
    C = min(T, CHUNK)
    n = T // C
    dt = q.dtype
    log_gamma = jnp.log1p(-jnp.exp2(-5.0 - jnp.arange(H, dtype=jnp.float32)))
    idx = jnp.arange(C, dtype=jnp.float32)
    rel = idx[:, None] - idx[None, :]
    decay_mask = jnp.where(rel >= 0,
                           jnp.exp(log_gamma[:, None, None] * jnp.maximum(rel, 0.0)),
                           0.0).astype(dt)
    xi = jnp.exp(log_gamma[:, None] * (idx + 1.0))[..., None].astype(dt)
    zeta = jnp.exp(log_gamma[:, None] * (C - 1.0 - idx))[..., None].astype(dt)
    g_chunk = jnp.exp(log_gamma * C)[:, None, None].astype(dt)

    def step(R, qkv):
        qc, kc, vc = qkv
        s = jnp.einsum('bhid,bhjd->bhij', qc, kc) * decay_mask
        o = (jnp.einsum('bhij,bhjv->bhiv', s, vc)
             + jnp.einsum('bhid,bhdv->bhiv', qc, R) * xi)
        R = R * g_chunk + jnp.einsum('bhjd,bhjv->bhdv', kc * zeta, vc)
        return R, o

    def split(t):
        return t.reshape(B, H, n, C, t.shape[-1]).transpose(2, 0, 1, 3, 4)

    R, o = lax.scan(step, R0.astype(dt), (split(q), split(k), split(v)))
    o = o.transpose(1, 2, 0, 3, 4).reshape(B, H, T, -1)
    return o, R


def mixer(h, conv_hist, R0, pos0, w_in, conv_w, conv_b, conv_ln_g, conv_ln_b,
          ret_gn_g, w_out):
    B, T, _ = h.shape
    p = h @ w_in
    a, b, q, k, v, g = jnp.split(
        p, [D_CONV, 2 * D_CONV, 2 * D_CONV + D_RET, 2 * D_CONV + 2 * D_RET,
            2 * D_CONV + 3 * D_RET], axis=-1)

    u = a * jax.nn.sigmoid(b)
    u_ext = jnp.concatenate([conv_hist.astype(u.dtype), u], axis=1)
    c = lax.conv_general_dilated(
        u_ext, conv_w[:, None, :].astype(u.dtype), window_strides=(1,), padding='VALID',
        dimension_numbers=('NWC', 'WIO', 'NWC'), feature_group_count=D_CONV) + conv_b
    new_hist = u_ext[:, -(CONV_WIDTH - 1):]
    c = jax.nn.silu(layernorm(c, conv_ln_g, conv_ln_b))

    heads = lambda t: t.reshape(B, T, N_RET_HEADS, HEAD_DIM).transpose(0, 2, 1, 3)
    pos = pos0 + jnp.arange(T)
    qh = rope(heads(q), pos)
    kh = rope(heads(k), pos) * (HEAD_DIM ** -0.5)
    o, R = retention(qh, kh, heads(v), R0)
    of = o.astype(jnp.float32)
    mu = jnp.mean(of, axis=-1, keepdims=True)
    var = jnp.mean(jnp.square(of - mu), axis=-1, keepdims=True)
    on = ((of - mu) * lax.rsqrt(var + EPS)).astype(h.dtype)
    r = on.transpose(0, 2, 1, 3).reshape(B, T, D_RET) * ret_gn_g
    r = jax.nn.silu(g) * r

    out = jnp.concatenate([c, r], axis=-1) @ w_out
    return out, new_hist, R


def layer(x, conv_hist, R0, pos0,
          ffn1_norm_pre, ffn1_w_gate, ffn1_w_up, ffn1_w_down, ffn1_norm_post,
          mix_norm_pre, w_in, conv_w, conv_b, conv_ln_g, conv_ln_b, ret_gn_g, w_out,
          mix_norm_post,
          ffn2_norm_pre, ffn2_w_gate, ffn2_w_up, ffn2_w_down, ffn2_norm_post):
    x = x + 0.5 * rmsnorm(swiglu(rmsnorm(x, ffn1_norm_pre), ffn1_w_gate, ffn1_w_up,
                                 ffn1_w_down), ffn1_norm_post)
    m, new_hist, R = mixer(rmsnorm(x, mix_norm_pre), conv_hist, R0, pos0, w_in, conv_w,
                           conv_b, conv_ln_g, conv_ln_b, ret_gn_g, w_out)
    x = x + rmsnorm(m, mix_norm_post)
    x = x + 0.5 * rmsnorm(swiglu(rmsnorm(x, ffn2_norm_pre), ffn2_w_gate, ffn2_w_up,
                                 ffn2_w_down), ffn2_norm_post)
    return x, new_hist, R


def setup_inputs(seed: int = 0) -> dict:
    key = jax.random.key(seed)
    ks = iter(jax.random.split(key, 32))
    nrm = lambda shape, s: jax.random.normal(next(ks), shape, jnp.float32) * s
    gain = lambda shape: 1.0 + nrm(shape, 0.02)
    L = DEPTH
    return {
        "x_prompt": nrm((BATCH, SEQ, D_MODEL), 1.0),
        "x_sample": nrm((DEC_BATCH, DEC_SEQ, D_MODEL), 1.0),
        "state_conv": nrm((L, DEC_BATCH, CONV_WIDTH - 1, D_CONV), 0.5),
        "state_ret": nrm((L, DEC_BATCH, N_RET_HEADS, HEAD_DIM, HEAD_DIM), 0.5),
        "ffn1_norm_pre": gain((L, D_MODEL)),
        "ffn1_w_gate": nrm((L, D_MODEL, D_FF), D_MODEL ** -0.5),
        "ffn1_w_up": nrm((L, D_MODEL, D_FF), D_MODEL ** -0.5),
        "ffn1_w_down": nrm((L, D_FF, D_MODEL), D_FF ** -0.5),
        "ffn1_norm_post": gain((L, D_MODEL)),
        "mix_norm_pre": gain((L, D_MODEL)),
        "w_in": nrm((L, D_MODEL, D_IN), D_MODEL ** -0.5),
        "conv_w": nrm((L, CONV_WIDTH, D_CONV), CONV_WIDTH ** -0.5),
        "conv_b": nrm((L, D_CONV), 0.02),
        "conv_ln_g": gain((L, D_CONV)),
        "conv_ln_b": nrm((L, D_CONV), 0.02),
        "ret_gn_g": gain((L, D_RET)),
        "w_out": nrm((L, D_MIX, D_MODEL), D_MIX ** -0.5),
        "mix_norm_post": gain((L, D_MODEL)),
        "ffn2_norm_pre": gain((L, D_MODEL)),
        "ffn2_w_gate": nrm((L, D_MODEL, D_FF), D_MODEL ** -0.5),
        "ffn2_w_up": nrm((L, D_MODEL, D_FF), D_MODEL ** -0.5),
        "ffn2_w_down": nrm((L, D_FF, D_MODEL), D_FF ** -0.5),
        "ffn2_norm_post": gain((L, D_MODEL)),
    }


def reference(x_prompt, x_sample, state_conv, state_ret,
              ffn1_norm_pre, ffn1_w_gate, ffn1_w_up, ffn1_w_down, ffn1_norm_post,
              mix_norm_pre, w_in, conv_w, conv_b, conv_ln_g, conv_ln_b, ret_gn_g, w_out,
              mix_norm_post,
              ffn2_norm_pre, ffn2_w_gate, ffn2_w_up, ffn2_w_down, ffn2_norm_post):
    yp, ys = x_prompt, x_sample
    Bp = x_prompt.shape[0]
    conv_p, ret_p, conv_s, ret_s = [], [], [], []
    for l in range(DEPTH):
        w = (ffn1_norm_pre[l], ffn1_w_gate[l], ffn1_w_up[l], ffn1_w_down[l],
             ffn1_norm_post[l], mix_norm_pre[l], w_in[l], conv_w[l], conv_b[l],
             conv_ln_g[l], conv_ln_b[l], ret_gn_g[l], w_out[l], mix_norm_post[l],
             ffn2_norm_pre[l], ffn2_w_gate[l], ffn2_w_up[l], ffn2_w_down[l],
             ffn2_norm_post[l])
        hist0 = jnp.zeros((Bp, CONV_WIDTH - 1, D_CONV), yp.dtype)
        R0 = jnp.zeros((Bp, N_RET_HEADS, HEAD_DIM, HEAD_DIM), yp.dtype)
        yp, hp, rp = layer(yp, hist0, R0, 0, *w)
        ys, hs, rs = layer(ys, state_conv[l], state_ret[l], PAST_LEN, *w)
        conv_p.append(hp); ret_p.append(rp); conv_s.append(hs); ret_s.append(rs)
    new_conv_prompt = jnp.stack(conv_p)
    new_ret_prompt = jnp.stack(ret_p)
    new_conv_sample = jnp.stack(conv_s)
    new_ret_sample = jnp.stack(ret_s)
    return (yp, ys, new_conv_prompt, new_ret_prompt, new_conv_sample, new_ret_sample)
```

```python
import functools
import math

import jax
import jax.numpy as jnp
from jax import lax
from jax.experimental import pallas as pl
from jax.experimental.pallas import tpu as pltpu

D_MODEL = 1024
D_FF = 2816
D_CONV = 512
D_RET = 512
N_RET_HEADS = 4
HEAD_DIM = 128
CONV_WIDTH = 31
HIST = CONV_WIDTH - 1
ROPE_THETA = 10000.0
EPS = 1e-6
D_IN = 2 * D_CONV + 4 * D_RET
PAST_LEN = 4096

V7X_SUBLANES = 8
V7X_VMEM_BYTES = 64 * 1024 * 1024

HIST_PAD = 32
CONV_ROW_BLOCK = 64
FFN_TILE = 512
MIX_TILE = 512
RET_CHUNK = 256
FF_CHUNK = 1408

BF16 = jnp.bfloat16
F32 = jnp.float32


def _vmem_limit(resident_bytes, tile_bytes, temp_bytes):
    need = resident_bytes + 2 * tile_bytes + temp_bytes
    return int(min(need * 5 // 4 + (4 << 20), V7X_VMEM_BYTES - (6 << 20)))


def _rmsnorm(x, g):
    ms = jnp.mean(x * x, axis=-1, keepdims=True)
    return x * lax.rsqrt(ms + EPS) * g


def _silu(x):
    return x * jax.nn.sigmoid(x)


def _resident(shape):
    nd = len(shape)
    return pl.BlockSpec(shape, lambda *_: (0,) * nd, pipeline_mode=pl.Buffered(1))


def _ffn_kernel(x_ref, gpre_ref, wg_ref, wu_ref, wd_ref, gpost_ref, o_ref):
    x = x_ref[...]
    hb = _rmsnorm(x, gpre_ref[...]).astype(BF16)
    acc = None
    for c in range(D_FF // FF_CHUNK):
        sl = slice(c * FF_CHUNK, (c + 1) * FF_CHUNK)
        g = jnp.dot(hb, wg_ref[:, sl], preferred_element_type=F32)
        u = jnp.dot(hb, wu_ref[:, sl], preferred_element_type=F32)
        a = (_silu(g) * u).astype(BF16)
        d = jnp.dot(a, wd_ref[sl, :], preferred_element_type=F32)
        acc = d if acc is None else acc + d
    o_ref[...] = x + 0.5 * _rmsnorm(acc, gpost_ref[...])


def _ffn(x2d, gpre, wg, wu, wd, gpost):
    rows = x2d.shape[0]
    tm = min(FFN_TILE, rows)
    assert rows % tm == 0
    tile = pl.BlockSpec((tm, D_MODEL), lambda i: (i, 0))
    weights = 3 * D_MODEL * D_FF * 2
    return pl.pallas_call(
        _ffn_kernel,
        grid=(rows // tm,),
        in_specs=[tile, _resident((1, D_MODEL)), _resident((D_MODEL, D_FF)),
                  _resident((D_MODEL, D_FF)), _resident((D_FF, D_MODEL)),
                  _resident((1, D_MODEL))],
        out_specs=tile,
        out_shape=jax.ShapeDtypeStruct((rows, D_MODEL), F32),
        compiler_params=pltpu.CompilerParams(
            dimension_semantics=("arbitrary",),
            vmem_limit_bytes=_vmem_limit(weights, 2 * tm * D_MODEL * 4,
                                         tm * (3 * FF_CHUNK + 2 * D_MODEL) * 4)),
        name="ffn",
    )(x2d, gpre, wg, wu, wd, gpost)


def _mixer_kernel(x_ref, cos_ref, sin_ref, hist0_ref, r0_ref, gpre_ref, win_ref, cw_ref,
                  cb_ref, lng_ref, lnb_ref, gng_ref, wout_ref, gpost_ref,
                  y_ref, hist_ref, rout_ref,
                  uext_ref, rstate_ref, mix_ref, dmask_ref, decay_ref, *, nb, tl, chunk):
    t = pl.program_id(1)
    log_gamma = [math.log1p(-(2.0 ** (-5.0 - h))) for h in range(N_RET_HEADS)]

    @pl.when(t == 0)
    def _init():
        uext_ref[:, HIST_PAD - HIST:HIST_PAD, :] = hist0_ref[...]
        rstate_ref[...] = r0_ref[...]
        ii = lax.broadcasted_iota(jnp.int32, (chunk, chunk), 0)
        jj = lax.broadcasted_iota(jnp.int32, (chunk, chunk), 1)
        rel = (ii - jj).astype(F32)
        row = lax.broadcasted_iota(jnp.int32, (chunk, HEAD_DIM), 0).astype(F32)
        for h in range(N_RET_HEADS):
            dmask_ref[h] = jnp.where(rel >= 0.0, jnp.exp(log_gamma[h] * jnp.maximum(rel, 0.0)), 0.0)
            decay_ref[h, 0] = jnp.exp(log_gamma[h] * (row + 1.0))
            decay_ref[h, 1] = jnp.exp(log_gamma[h] * (chunk - 1.0 - row))

    m = nb * tl
    x = x_ref[...].reshape(m, D_MODEL)
    hb = _rmsnorm(x, gpre_ref[...]).astype(BF16)
    p = jnp.dot(hb, win_ref[...], preferred_element_type=F32)

    u = p[:, :D_CONV] * jax.nn.sigmoid(p[:, D_CONV:2 * D_CONV])
    for s in range(nb):
        uext_ref[s, HIST_PAD:HIST_PAD + tl, :] = u[s * tl:(s + 1) * tl]
    rb = min(CONV_ROW_BLOCK, tl)
    base = HIST_PAD - HIST
    for s in range(nb):
        for r0 in range(0, tl, rb):
            acc = jnp.broadcast_to(cb_ref[...], (rb, D_CONV))
            for j in range(CONV_WIDTH):
                acc = acc + cw_ref[j:j + 1, :] * uext_ref[s, r0 + base + j:r0 + base + j + rb, :]
            mu = jnp.mean(acc, axis=-1, keepdims=True)
            dc = acc - mu
            var = jnp.mean(dc * dc, axis=-1, keepdims=True)
            c = _silu(dc * lax.rsqrt(var + EPS) * lng_ref[...] + lnb_ref[...])
            mix_ref[s * tl + r0:s * tl + r0 + rb, 0:D_CONV] = c.astype(BF16)
        new_hist = uext_ref[s, tl + base:tl + HIST_PAD, :]
        hist_ref[s] = new_hist
        uext_ref[s, base:HIST_PAD, :] = new_hist

    cos = cos_ref[...]
    sin = sin_ref[...]
    q0, k0, v0, g0 = (2 * D_CONV + i * D_RET for i in range(4))
    for s in range(nb):
        rows = slice(s * tl, (s + 1) * tl)
        for h in range(N_RET_HEADS):
            hs = slice(h * HEAD_DIM, (h + 1) * HEAD_DIM)
            lane = lambda off: slice(off + h * HEAD_DIM, off + (h + 1) * HEAD_DIM)
            q = p[rows, lane(q0)]
            k = p[rows, lane(k0)]
            v = p[rows, lane(v0)].astype(BF16)
            gate = p[rows, lane(g0)]
            qr = (q * cos + pltpu.roll(q, HEAD_DIM // 2, 1) * sin).astype(BF16)
            kr = (k * cos + pltpu.roll(k, HEAD_DIM // 2, 1) * sin) * (HEAD_DIM ** -0.5)
            state = rstate_ref[s, h]
            g_chunk = math.exp(log_gamma[h] * chunk)
            for ci in range(tl // chunk):
                cs = slice(ci * chunk, (ci + 1) * chunk)
                qc, kc, vc = qr[cs], kr[cs], v[cs]
                sc = lax.dot_general(qc, kc.astype(BF16), (((1,), (1,)), ((), ())),
                                     preferred_element_type=F32) * dmask_ref[h]
                o = (jnp.dot(sc.astype(BF16), vc, preferred_element_type=F32)
                     + jnp.dot(qc, state.astype(BF16), preferred_element_type=F32) * decay_ref[h, 0])
                kz = (kc * decay_ref[h, 1]).astype(BF16)
                state = state * g_chunk + lax.dot_general(
                    kz, vc, (((0,), (0,)), ((), ())), preferred_element_type=F32)
                mu = jnp.mean(o, axis=-1, keepdims=True)
                do = o - mu
                var = jnp.mean(do * do, axis=-1, keepdims=True)
                on = do * lax.rsqrt(var + EPS)
                r = _silu(gate[cs]) * (on * gng_ref[:, hs])
                mix_ref[s * tl + ci * chunk:s * tl + (ci + 1) * chunk,
                        D_CONV + h * HEAD_DIM:D_CONV + (h + 1) * HEAD_DIM] = r.astype(BF16)
            rstate_ref[s, h] = state
            rout_ref[s, h] = state

    out = jnp.dot(mix_ref[...], wout_ref[...], preferred_element_type=F32)
    y_ref[...] = (x + _rmsnorm(out, gpost_ref[...])).reshape(nb, tl, D_MODEL)


def _mixer(x, cos, sin, hist0, r0, gpre, win, cw, cb, lng, lnb, gng, wout, gpost, *, nb, tl):
    bsz, seq, _ = x.shape
    assert bsz % nb == 0 and seq % tl == 0 and tl % V7X_SUBLANES == 0
    chunk = min(RET_CHUNK, tl)
    assert tl % chunk == 0
    m = nb * tl
    body = functools.partial(_mixer_kernel, nb=nb, tl=tl, chunk=chunk)
    weights = (D_MODEL * D_IN + D_MODEL * D_MODEL) * 2
    scratch = (nb * (tl + HIST_PAD) * D_CONV * 4 + nb * N_RET_HEADS * HEAD_DIM * HEAD_DIM * 4
               + m * D_MODEL * 2 + N_RET_HEADS * chunk * (chunk + 2 * HEAD_DIM) * 4)
    return pl.pallas_call(
        body,
        grid=(bsz // nb, seq // tl),
        in_specs=[
            pl.BlockSpec((nb, tl, D_MODEL), lambda b, t: (b, t, 0)),
            pl.BlockSpec((tl, HEAD_DIM), lambda b, t: (t, 0)),
            pl.BlockSpec((tl, HEAD_DIM), lambda b, t: (t, 0)),
            pl.BlockSpec((nb, HIST, D_CONV), lambda b, t: (b, 0, 0)),
            pl.BlockSpec((nb, N_RET_HEADS, HEAD_DIM, HEAD_DIM), lambda b, t: (b, 0, 0, 0)),
            _resident((1, D_MODEL)), _resident((D_MODEL, D_IN)), _resident((CONV_WIDTH, D_CONV)),
            _resident((1, D_CONV)), _resident((1, D_CONV)), _resident((1, D_CONV)),
            _resident((1, D_RET)), _resident((D_MODEL, D_MODEL)), _resident((1, D_MODEL)),
        ],
        out_specs=[
            pl.BlockSpec((nb, tl, D_MODEL), lambda b, t: (b, t, 0)),
            pl.BlockSpec((nb, HIST, D_CONV), lambda b, t: (b, 0, 0)),
            pl.BlockSpec((nb, N_RET_HEADS, HEAD_DIM, HEAD_DIM), lambda b, t: (b, 0, 0, 0)),
        ],
        out_shape=[
            jax.ShapeDtypeStruct((bsz, seq, D_MODEL), F32),
            jax.ShapeDtypeStruct((bsz, HIST, D_CONV), F32),
            jax.ShapeDtypeStruct((bsz, N_RET_HEADS, HEAD_DIM, HEAD_DIM), F32),
        ],
        scratch_shapes=[
            pltpu.VMEM((nb, tl + HIST_PAD, D_CONV), F32),
            pltpu.VMEM((nb, N_RET_HEADS, HEAD_DIM, HEAD_DIM), F32),
            pltpu.VMEM((m, D_MODEL), BF16),
            pltpu.VMEM((N_RET_HEADS, chunk, chunk), F32),
            pltpu.VMEM((N_RET_HEADS, 2, chunk, HEAD_DIM), F32),
        ],
        compiler_params=pltpu.CompilerParams(
            dimension_semantics=("arbitrary", "arbitrary"),
            vmem_limit_bytes=_vmem_limit(weights + scratch, 2 * m * D_MODEL * 4,
                                         m * (D_IN + 3 * D_MODEL) * 4)),
        name="mixer",
    )(x, cos, sin, hist0, r0, gpre, win, cw, cb, lng, lnb, gng, wout, gpost)


def _rope_tables(pos0, n):
    half = HEAD_DIM // 2
    inv = ROPE_THETA ** (-jnp.arange(half, dtype=F32) / half)
    ang = (pos0 + jnp.arange(n)).astype(F32)[:, None] * inv[None, :]
    cos, sin = jnp.cos(ang), jnp.sin(ang)
    return jnp.concatenate([cos, cos], axis=-1), jnp.concatenate([-sin, sin], axis=-1)


def _layer(x, hist0, r0, pos0, w, *, nb, tl):
    bsz, seq, _ = x.shape
    row = lambda a: a.reshape(1, -1)
    x2 = _ffn(x.reshape(bsz * seq, D_MODEL), row(w["ffn1_norm_pre"]), w["ffn1_w_gate"],
              w["ffn1_w_up"], w["ffn1_w_down"], row(w["ffn1_norm_post"]))
    cos, sin = _rope_tables(pos0, seq)
    y, hist, rstate = _mixer(
        x2.reshape(bsz, seq, D_MODEL), cos, sin, hist0, r0, row(w["mix_norm_pre"]), w["w_in"],
        w["conv_w"], row(w["conv_b"]), row(w["conv_ln_g"]), row(w["conv_ln_b"]),
        row(w["ret_gn_g"]), w["w_out"], row(w["mix_norm_post"]), nb=nb, tl=tl)
    y2 = _ffn(y.reshape(bsz * seq, D_MODEL), row(w["ffn2_norm_pre"]), w["ffn2_w_gate"],
              w["ffn2_w_up"], w["ffn2_w_down"], row(w["ffn2_norm_post"]))
    return y2.reshape(bsz, seq, D_MODEL), hist, rstate


_MATMUL_WEIGHTS = ("ffn1_w_gate", "ffn1_w_up", "ffn1_w_down", "w_in", "w_out",
                   "ffn2_w_gate", "ffn2_w_up", "ffn2_w_down")


def kernel(x_prompt, x_sample, state_conv, state_ret, ffn1_norm_pre, ffn1_w_gate, ffn1_w_up, ffn1_w_down, ffn1_norm_post, mix_norm_pre, w_in, conv_w, conv_b, conv_ln_g, conv_ln_b, ret_gn_g, w_out, mix_norm_post, ffn2_norm_pre, ffn2_w_gate, ffn2_w_up, ffn2_w_down, ffn2_norm_post):
    params = dict(
        ffn1_norm_pre=ffn1_norm_pre, ffn1_w_gate=ffn1_w_gate, ffn1_w_up=ffn1_w_up,
        ffn1_w_down=ffn1_w_down, ffn1_norm_post=ffn1_norm_post, mix_norm_pre=mix_norm_pre,
        w_in=w_in, conv_w=conv_w, conv_b=conv_b, conv_ln_g=conv_ln_g, conv_ln_b=conv_ln_b,
        ret_gn_g=ret_gn_g, w_out=w_out, mix_norm_post=mix_norm_post,
        ffn2_norm_pre=ffn2_norm_pre, ffn2_w_gate=ffn2_w_gate, ffn2_w_up=ffn2_w_up,
        ffn2_w_down=ffn2_w_down, ffn2_norm_post=ffn2_norm_post)
    depth = ffn1_norm_pre.shape[0]
    bp, sp, _ = x_prompt.shape
    bs, ss, _ = x_sample.shape
    yp, ys = x_prompt, x_sample
    conv_p, ret_p, conv_s, ret_s = [], [], [], []
    for l in range(depth):
        w = {k: (v[l].astype(BF16) if k in _MATMUL_WEIGHTS else v[l]) for k, v in params.items()}
        hist0 = jnp.zeros((bp, HIST, D_CONV), F32)
        r0 = jnp.zeros((bp, N_RET_HEADS, HEAD_DIM, HEAD_DIM), F32)
        yp, hp, rp = _layer(yp, hist0, r0, 0, w, nb=1, tl=min(MIX_TILE, sp))
        ys, hs, rs = _layer(ys, state_conv[l], state_ret[l], PAST_LEN, w, nb=bs, tl=ss)
        conv_p.append(hp); ret_p.append(rp); conv_s.append(hs); ret_s.append(rs)
    return (yp, ys, jnp.stack(conv_p), jnp.stack(ret_p), jnp.stack(conv_s), jnp.stack(ret_s))
```

```python
import functools
import math

import numpy as np

import jax
import jax.numpy as jnp
from jax import lax
from jax.experimental import pallas as pl
from jax.experimental.pallas import tpu as pltpu

D_MODEL = 1024
D_FF = 2816
D_CONV = 512
D_RET = 512
N_RET_HEADS = 4
HEAD_DIM = 128
CONV_WIDTH = 31
HIST = CONV_WIDTH - 1
ROPE_THETA = 10000.0
EPS = 1e-6
D_IN = 2 * D_CONV + 4 * D_RET
PAST_LEN = 4096

V7X_SUBLANES = 8
V7X_LANES = 128
LANE_BLOCKS = D_CONV // V7X_LANES
V7X_VMEM_BYTES = 64 * 1024 * 1024

HIST_PAD = 32
CONV_ROW_BLOCK = 64
FFN_TILE = 512
MIX_TILE = 512
RET_CHUNK = 256
FF_CHUNK = 1408

BF16 = jnp.bfloat16
F32 = jnp.float32


def _vmem_limit(resident_bytes, tile_bytes, temp_bytes):
    need = resident_bytes + 2 * tile_bytes + temp_bytes
    return int(min(need * 5 // 4 + (4 << 20), V7X_VMEM_BYTES - (6 << 20)))


def _rmsnorm(x, g):
    ms = jnp.mean(x * x, axis=-1, keepdims=True)
    return x * lax.rsqrt(ms + EPS) * g


def _lanes(block):
    return slice(block * V7X_LANES, (block + 1) * V7X_LANES)


def _silu(x):
    return x * jax.nn.sigmoid(x)


def _resident(shape):
    nd = len(shape)
    return pl.BlockSpec(shape, lambda *_: (0,) * nd, pipeline_mode=pl.Buffered(1))


def _ffn_kernel(x_ref, gpre_ref, wg_ref, wu_ref, wd_ref, gpost_ref, o_ref):
    x = x_ref[...]
    hb = _rmsnorm(x, gpre_ref[...]).astype(BF16)
    acc = None
    for c in range(D_FF // FF_CHUNK):
        sl = slice(c * FF_CHUNK, (c + 1) * FF_CHUNK)
        g = jnp.dot(hb, wg_ref[:, sl], preferred_element_type=F32)
        u = jnp.dot(hb, wu_ref[:, sl], preferred_element_type=F32)
        a = (_silu(g) * u).astype(BF16)
        d = jnp.dot(a, wd_ref[sl, :], preferred_element_type=F32)
        acc = d if acc is None else acc + d
    o_ref[...] = x + 0.5 * _rmsnorm(acc, gpost_ref[...])


def _ffn(x2d, gpre, wg, wu, wd, gpost):
    rows = x2d.shape[0]
    tm = min(FFN_TILE, rows)
    assert rows % tm == 0
    tile = pl.BlockSpec((tm, D_MODEL), lambda i: (i, 0))
    weights = 3 * D_MODEL * D_FF * 2
    return pl.pallas_call(
        _ffn_kernel,
        grid=(rows // tm,),
        in_specs=[tile, _resident((1, D_MODEL)), _resident((D_MODEL, D_FF)),
                  _resident((D_MODEL, D_FF)), _resident((D_FF, D_MODEL)),
                  _resident((1, D_MODEL))],
        out_specs=tile,
        out_shape=jax.ShapeDtypeStruct((rows, D_MODEL), F32),
        compiler_params=pltpu.CompilerParams(
            dimension_semantics=("arbitrary",),
            vmem_limit_bytes=_vmem_limit(weights, 2 * tm * D_MODEL * 4,
                                         tm * (3 * FF_CHUNK + 2 * D_MODEL) * 4)),
        name="ffn",
    )(x2d, gpre, wg, wu, wd, gpost)


def _mixer_kernel(x_ref, cos_ref, sin_ref, hist0_ref, r0_ref, gpre_ref, win_ref, cw_ref,
                  cb_ref, lng_ref, lnb_ref, gng_ref, wout_ref, gpost_ref,
                  y_ref, hist_ref, rout_ref,
                  uext_ref, rstate_ref, mix_ref, dmask_ref, decay_ref, *, nb, tl, chunk):
    t = pl.program_id(1)
    log_gamma = [math.log1p(-(2.0 ** (-5.0 - h))) for h in range(N_RET_HEADS)]

    @pl.when(t == 0)
    def _init():
        for lb in range(LANE_BLOCKS):
            uext_ref[:, lb, HIST_PAD - HIST:HIST_PAD, :] = hist0_ref[:, :, _lanes(lb)]
        rstate_ref[...] = r0_ref[...]
        ii = lax.broadcasted_iota(jnp.int32, (chunk, chunk), 0)
        jj = lax.broadcasted_iota(jnp.int32, (chunk, chunk), 1)
        rel = (ii - jj).astype(F32)
        row = lax.broadcasted_iota(jnp.int32, (chunk, HEAD_DIM), 0).astype(F32)
        for h in range(N_RET_HEADS):
            dmask_ref[h] = jnp.where(rel >= 0.0, jnp.exp(log_gamma[h] * jnp.maximum(rel, 0.0)), 0.0)
            decay_ref[h, 0] = jnp.exp(log_gamma[h] * (row + 1.0))
            decay_ref[h, 1] = jnp.exp(log_gamma[h] * (chunk - 1.0 - row))

    m = nb * tl
    x = x_ref[...].reshape(m, D_MODEL)
    hb = _rmsnorm(x, gpre_ref[...]).astype(BF16)
    p = jnp.dot(hb, win_ref[...], preferred_element_type=F32)

    u = p[:, :D_CONV] * jax.nn.sigmoid(p[:, D_CONV:2 * D_CONV])
    for s in range(nb):
        for lb in range(LANE_BLOCKS):
            uext_ref[s, lb, HIST_PAD:HIST_PAD + tl, :] = u[s * tl:(s + 1) * tl, _lanes(lb)]
    rb = min(CONV_ROW_BLOCK, tl)
    base = HIST_PAD - HIST
    for s in range(nb):
        for r0 in range(0, tl, rb):
            taps = []
            for lb in range(LANE_BLOCKS):
                acc = jnp.broadcast_to(cb_ref[:, _lanes(lb)], (rb, V7X_LANES))
                for j in range(CONV_WIDTH):
                    acc = acc + (cw_ref[j:j + 1, _lanes(lb)]
                                 * uext_ref[s, lb, r0 + base + j:r0 + base + j + rb, :])
                taps.append(acc)
            acc = jnp.concatenate(taps, axis=-1)
            mu = jnp.mean(acc, axis=-1, keepdims=True)
            dc = acc - mu
            var = jnp.mean(dc * dc, axis=-1, keepdims=True)
            c = _silu(dc * lax.rsqrt(var + EPS) * lng_ref[...] + lnb_ref[...])
            mix_ref[s * tl + r0:s * tl + r0 + rb, 0:D_CONV] = c.astype(BF16)
        for lb in range(LANE_BLOCKS):
            new_hist = uext_ref[s, lb, tl + base:tl + HIST_PAD, :]
            hist_ref[s, :, _lanes(lb)] = new_hist
            uext_ref[s, lb, base:HIST_PAD, :] = new_hist

    cos = cos_ref[...]
    sin = sin_ref[...]
    q0, k0, v0, g0 = (2 * D_CONV + i * D_RET for i in range(4))
    for s in range(nb):
        rows = slice(s * tl, (s + 1) * tl)
        for h in range(N_RET_HEADS):
            hs = slice(h * HEAD_DIM, (h + 1) * HEAD_DIM)
            lane = lambda off: slice(off + h * HEAD_DIM, off + (h + 1) * HEAD_DIM)
            q = p[rows, lane(q0)]
            k = p[rows, lane(k0)]
            v = p[rows, lane(v0)].astype(BF16)
            gate = p[rows, lane(g0)]
            qr = (q * cos + pltpu.roll(q, HEAD_DIM // 2, 1) * sin).astype(BF16)
            kr = (k * cos + pltpu.roll(k, HEAD_DIM // 2, 1) * sin) * (HEAD_DIM ** -0.5)
            state = rstate_ref[s, h]
            g_chunk = math.exp(log_gamma[h] * chunk)
            for ci in range(tl // chunk):
                cs = slice(ci * chunk, (ci + 1) * chunk)
                qc, kc, vc = qr[cs], kr[cs], v[cs]
                sc = lax.dot_general(qc, kc.astype(BF16), (((1,), (1,)), ((), ())),
                                     preferred_element_type=F32) * dmask_ref[h]
                o = (jnp.dot(sc.astype(BF16), vc, preferred_element_type=F32)
                     + jnp.dot(qc, state.astype(BF16), preferred_element_type=F32) * decay_ref[h, 0])
                kz = (kc * decay_ref[h, 1]).astype(BF16)
                state = state * g_chunk + lax.dot_general(
                    kz, vc, (((0,), (0,)), ((), ())), preferred_element_type=F32)
                mu = jnp.mean(o, axis=-1, keepdims=True)
                do = o - mu
                var = jnp.mean(do * do, axis=-1, keepdims=True)
                on = do * lax.rsqrt(var + EPS)
                r = _silu(gate[cs]) * (on * gng_ref[:, hs])
                mix_ref[s * tl + ci * chunk:s * tl + (ci + 1) * chunk,
                        D_CONV + h * HEAD_DIM:D_CONV + (h + 1) * HEAD_DIM] = r.astype(BF16)
            rstate_ref[s, h] = state
            rout_ref[s, h] = state

    out = jnp.dot(mix_ref[...], wout_ref[...], preferred_element_type=F32)
    y_ref[...] = (x + _rmsnorm(out, gpost_ref[...])).reshape(nb, tl, D_MODEL)


def _mixer(x, cos, sin, hist0, r0, gpre, win, cw, cb, lng, lnb, gng, wout, gpost, *, nb, tl):
    bsz, seq, _ = x.shape
    assert bsz % nb == 0 and seq % tl == 0 and tl % V7X_SUBLANES == 0
    chunk = min(RET_CHUNK, tl)
    assert tl % chunk == 0
    m = nb * tl
    body = functools.partial(_mixer_kernel, nb=nb, tl=tl, chunk=chunk)
    weights = (D_MODEL * D_IN + D_MODEL * D_MODEL) * 2
    scratch = (nb * (tl + HIST_PAD) * D_CONV * 4 + nb * N_RET_HEADS * HEAD_DIM * HEAD_DIM * 4
               + m * D_MODEL * 2 + N_RET_HEADS * chunk * (chunk + 2 * HEAD_DIM) * 4)
    return pl.pallas_call(
        body,
        grid=(bsz // nb, seq // tl),
        in_specs=[
            pl.BlockSpec((nb, tl, D_MODEL), lambda b, t: (b, t, 0)),
            pl.BlockSpec((tl, HEAD_DIM), lambda b, t: (t, 0)),
            pl.BlockSpec((tl, HEAD_DIM), lambda b, t: (t, 0)),
            pl.BlockSpec((nb, HIST, D_CONV), lambda b, t: (b, 0, 0)),
            pl.BlockSpec((nb, N_RET_HEADS, HEAD_DIM, HEAD_DIM), lambda b, t: (b, 0, 0, 0)),
            _resident((1, D_MODEL)), _resident((D_MODEL, D_IN)), _resident((CONV_WIDTH, D_CONV)),
            _resident((1, D_CONV)), _resident((1, D_CONV)), _resident((1, D_CONV)),
            _resident((1, D_RET)), _resident((D_MODEL, D_MODEL)), _resident((1, D_MODEL)),
        ],
        out_specs=[
            pl.BlockSpec((nb, tl, D_MODEL), lambda b, t: (b, t, 0)),
            pl.BlockSpec((nb, HIST, D_CONV), lambda b, t: (b, 0, 0)),
            pl.BlockSpec((nb, N_RET_HEADS, HEAD_DIM, HEAD_DIM), lambda b, t: (b, 0, 0, 0)),
        ],
        out_shape=[
            jax.ShapeDtypeStruct((bsz, seq, D_MODEL), F32),
            jax.ShapeDtypeStruct((bsz, HIST, D_CONV), F32),
            jax.ShapeDtypeStruct((bsz, N_RET_HEADS, HEAD_DIM, HEAD_DIM), F32),
        ],
        scratch_shapes=[
            pltpu.VMEM((nb, LANE_BLOCKS, tl + HIST_PAD, V7X_LANES), F32),
            pltpu.VMEM((nb, N_RET_HEADS, HEAD_DIM, HEAD_DIM), F32),
            pltpu.VMEM((m, D_MODEL), BF16),
            pltpu.VMEM((N_RET_HEADS, chunk, chunk), F32),
            pltpu.VMEM((N_RET_HEADS, 2, chunk, HEAD_DIM), F32),
        ],
        compiler_params=pltpu.CompilerParams(
            dimension_semantics=("arbitrary", "arbitrary"),
            vmem_limit_bytes=_vmem_limit(weights + scratch, 2 * m * D_MODEL * 4,
                                         m * (D_IN + 3 * D_MODEL) * 4)),
        name="mixer",
    )(x, cos, sin, hist0, r0, gpre, win, cw, cb, lng, lnb, gng, wout, gpost)


def _rope_tables(pos0, n):
    half = HEAD_DIM // 2
    inv = ROPE_THETA ** (-np.arange(half, dtype=np.float64) / half)
    ang = (pos0 + np.arange(n, dtype=np.float64))[:, None] * inv[None, :]
    cos, sin = np.cos(ang), np.sin(ang)
    return (np.concatenate([cos, cos], axis=-1).astype(np.float32),
            np.concatenate([-sin, sin], axis=-1).astype(np.float32))


def _layer(x, hist0, r0, pos0, w, *, nb, tl):
    bsz, seq, _ = x.shape
    row = lambda a: a.reshape(1, -1)
    x2 = _ffn(x.reshape(bsz * seq, D_MODEL), row(w["ffn1_norm_pre"]), w["ffn1_w_gate"],
              w["ffn1_w_up"], w["ffn1_w_down"], row(w["ffn1_norm_post"]))
    cos, sin = _rope_tables(pos0, seq)
    y, hist, rstate = _mixer(
        x2.reshape(bsz, seq, D_MODEL), cos, sin, hist0, r0, row(w["mix_norm_pre"]), w["w_in"],
        w["conv_w"], row(w["conv_b"]), row(w["conv_ln_g"]), row(w["conv_ln_b"]),
        row(w["ret_gn_g"]), w["w_out"], row(w["mix_norm_post"]), nb=nb, tl=tl)
    y2 = _ffn(y.reshape(bsz * seq, D_MODEL), row(w["ffn2_norm_pre"]), w["ffn2_w_gate"],
              w["ffn2_w_up"], w["ffn2_w_down"], row(w["ffn2_norm_post"]))
    return y2.reshape(bsz, seq, D_MODEL), hist, rstate


_MATMUL_WEIGHTS = ("ffn1_w_gate", "ffn1_w_up", "ffn1_w_down", "w_in", "w_out",
                   "ffn2_w_gate", "ffn2_w_up", "ffn2_w_down")


def kernel(x_prompt, x_sample, state_conv, state_ret, ffn1_norm_pre, ffn1_w_gate, ffn1_w_up, ffn1_w_down, ffn1_norm_post, mix_norm_pre, w_in, conv_w, conv_b, conv_ln_g, conv_ln_b, ret_gn_g, w_out, mix_norm_post, ffn2_norm_pre, ffn2_w_gate, ffn2_w_up, ffn2_w_down, ffn2_norm_post):
    params = dict(
        ffn1_norm_pre=ffn1_norm_pre, ffn1_w_gate=ffn1_w_gate, ffn1_w_up=ffn1_w_up,
        ffn1_w_down=ffn1_w_down, ffn1_norm_post=ffn1_norm_post, mix_norm_pre=mix_norm_pre,
        w_in=w_in, conv_w=conv_w, conv_b=conv_b, conv_ln_g=conv_ln_g, conv_ln_b=conv_ln_b,
        ret_gn_g=ret_gn_g, w_out=w_out, mix_norm_post=mix_norm_post,
        ffn2_norm_pre=ffn2_norm_pre, ffn2_w_gate=ffn2_w_gate, ffn2_w_up=ffn2_w_up,
        ffn2_w_down=ffn2_w_down, ffn2_norm_post=ffn2_norm_post)
    depth = ffn1_norm_pre.shape[0]
    bp, sp, _ = x_prompt.shape
    bs, ss, _ = x_sample.shape
    yp, ys = x_prompt, x_sample
    conv_p, ret_p, conv_s, ret_s = [], [], [], []
    for l in range(depth):
        w = {k: (v[l].astype(BF16) if k in _MATMUL_WEIGHTS else v[l]) for k, v in params.items()}
        hist0 = jnp.zeros((bp, HIST, D_CONV), F32)
        r0 = jnp.zeros((bp, N_RET_HEADS, HEAD_DIM, HEAD_DIM), F32)
        yp, hp, rp = _layer(yp, hist0, r0, 0, w, nb=1, tl=min(MIX_TILE, sp))
        ys, hs, rs = _layer(ys, state_conv[l], state_ret[l], PAST_LEN, w, nb=bs, tl=ss)
        conv_p.append(hp); ret_p.append(rp); conv_s.append(hs); ret_s.append(rs)
    return (yp, ys, jnp.stack(conv_p), jnp.stack(ret_p), jnp.stack(conv_s), jnp.stack(ret_s))
```

```python
import functools
import math

import numpy as np

import jax
import jax.numpy as jnp
from jax import lax
from jax.experimental import pallas as pl
from jax.experimental.pallas import tpu as pltpu

D_MODEL = 1024
D_FF = 2816
D_CONV = 512
D_RET = 512
N_RET_HEADS = 4
HEAD_DIM = 128
CONV_WIDTH = 31
HIST = CONV_WIDTH - 1
ROPE_THETA = 10000.0
EPS = 1e-6
D_IN = 2 * D_CONV + 4 * D_RET
PAST_LEN = 4096

V7X_SUBLANES = 8
V7X_LANES = 128
LANE_BLOCKS = D_CONV // V7X_LANES
V7X_VMEM_BYTES = 64 * 1024 * 1024

HIST_PAD = 32
CONV_ROW_BLOCK = 64
FFN_TILE = 1024
MIX_TILE = 512
RET_CHUNK = 256
V7X_MXU_DIM = 256
FF_SPLITS = (0, 6 * V7X_MXU_DIM, D_FF)
FF_CHUNK = max(b - a for a, b in zip(FF_SPLITS[:-1], FF_SPLITS[1:]))
assert all((b - a) % V7X_MXU_DIM == 0 for a, b in zip(FF_SPLITS[:-1], FF_SPLITS[1:]))

BF16 = jnp.bfloat16
F32 = jnp.float32


def _vmem_limit(resident_bytes, tile_bytes, temp_bytes):
    need = resident_bytes + 2 * tile_bytes + temp_bytes
    return int(min(need * 5 // 4 + (4 << 20), V7X_VMEM_BYTES - (6 << 20)))


def _rmsnorm(x, g):
    ms = jnp.mean(x * x, axis=-1, keepdims=True)
    return x * lax.rsqrt(ms + EPS) * g


def _lanes(block):
    return slice(block * V7X_LANES, (block + 1) * V7X_LANES)


def _silu(x):
    return x * jax.nn.sigmoid(x)


def _resident(shape):
    nd = len(shape)
    return pl.BlockSpec(shape, lambda *_: (0,) * nd, pipeline_mode=pl.Buffered(1))


def _ffn_kernel(x_ref, gpre_ref, wg_ref, wu_ref, wd_ref, gpost_ref, o_ref):
    x = x_ref[...]
    hb = _rmsnorm(x, gpre_ref[...]).astype(BF16)
    acc = None
    for lo, hi in zip(FF_SPLITS[:-1], FF_SPLITS[1:]):
        sl = slice(lo, hi)
        g = jnp.dot(hb, wg_ref[:, sl], preferred_element_type=F32)
        u = jnp.dot(hb, wu_ref[:, sl], preferred_element_type=F32)
        a = (_silu(g) * u).astype(BF16)
        d = jnp.dot(a, wd_ref[sl, :], preferred_element_type=F32)
        acc = d if acc is None else acc + d
    o_ref[...] = x + 0.5 * _rmsnorm(acc, gpost_ref[...])


def _ffn(x2d, gpre, wg, wu, wd, gpost):
    rows = x2d.shape[0]
    tm = min(FFN_TILE, rows)
    assert rows % tm == 0
    tile = pl.BlockSpec((tm, D_MODEL), lambda i: (i, 0))
    weights = 3 * D_MODEL * D_FF * 2
    return pl.pallas_call(
        _ffn_kernel,
        grid=(rows // tm,),
        in_specs=[tile, _resident((1, D_MODEL)), _resident((D_MODEL, D_FF)),
                  _resident((D_MODEL, D_FF)), _resident((D_FF, D_MODEL)),
                  _resident((1, D_MODEL))],
        out_specs=tile,
        out_shape=jax.ShapeDtypeStruct((rows, D_MODEL), F32),
        compiler_params=pltpu.CompilerParams(
            dimension_semantics=("arbitrary",),
            vmem_limit_bytes=_vmem_limit(weights, 2 * tm * D_MODEL * 4,
                                         tm * (3 * FF_CHUNK + 2 * D_MODEL) * 4)),
        name="ffn",
    )(x2d, gpre, wg, wu, wd, gpost)


def _mixer_kernel(x_ref, cos_ref, sin_ref, hist0_ref, r0_ref, gpre_ref, win_ref, cw_ref,
                  cb_ref, lng_ref, lnb_ref, gng_ref, wout_ref, gpost_ref,
                  y_ref, hist_ref, rout_ref,
                  uext_ref, rstate_ref, mix_ref, dmask_ref, decay_ref, *, nb, tl, chunk):
    t = pl.program_id(1)
    log_gamma = [math.log1p(-(2.0 ** (-5.0 - h))) for h in range(N_RET_HEADS)]

    @pl.when(t == 0)
    def _init():
        for lb in range(LANE_BLOCKS):
            uext_ref[:, lb, HIST_PAD - HIST:HIST_PAD, :] = hist0_ref[:, :, _lanes(lb)]
        rstate_ref[...] = r0_ref[...]
        ii = lax.broadcasted_iota(jnp.int32, (chunk, chunk), 0)
        jj = lax.broadcasted_iota(jnp.int32, (chunk, chunk), 1)
        rel = (ii - jj).astype(F32)
        row = lax.broadcasted_iota(jnp.int32, (chunk, HEAD_DIM), 0).astype(F32)
        for h in range(N_RET_HEADS):
            dmask_ref[h] = jnp.where(rel >= 0.0, jnp.exp(log_gamma[h] * jnp.maximum(rel, 0.0)), 0.0)
            decay_ref[h, 0] = jnp.exp(log_gamma[h] * (row + 1.0))
            decay_ref[h, 1] = jnp.exp(log_gamma[h] * (chunk - 1.0 - row))

    m = nb * tl
    x = x_ref[...].reshape(m, D_MODEL)
    hb = _rmsnorm(x, gpre_ref[...]).astype(BF16)
    p = jnp.dot(hb, win_ref[...], preferred_element_type=F32)

    u = p[:, :D_CONV] * jax.nn.sigmoid(p[:, D_CONV:2 * D_CONV])
    for s in range(nb):
        for lb in range(LANE_BLOCKS):
            uext_ref[s, lb, HIST_PAD:HIST_PAD + tl, :] = u[s * tl:(s + 1) * tl, _lanes(lb)]
    rb = min(CONV_ROW_BLOCK, tl)
    base = HIST_PAD - HIST
    for s in range(nb):
        for r0 in range(0, tl, rb):
            taps = []
            for lb in range(LANE_BLOCKS):
                acc = jnp.broadcast_to(cb_ref[:, _lanes(lb)], (rb, V7X_LANES))
                for j in range(CONV_WIDTH):
                    acc = acc + (cw_ref[j:j + 1, _lanes(lb)]
                                 * uext_ref[s, lb, r0 + base + j:r0 + base + j + rb, :])
                taps.append(acc)
            acc = jnp.concatenate(taps, axis=-1)
            mu = jnp.mean(acc, axis=-1, keepdims=True)
            dc = acc - mu
            var = jnp.mean(dc * dc, axis=-1, keepdims=True)
            c = _silu(dc * lax.rsqrt(var + EPS) * lng_ref[...] + lnb_ref[...])
            mix_ref[s * tl + r0:s * tl + r0 + rb, 0:D_CONV] = c.astype(BF16)
        for lb in range(LANE_BLOCKS):
            new_hist = uext_ref[s, lb, tl + base:tl + HIST_PAD, :]
            hist_ref[s, :, _lanes(lb)] = new_hist
            uext_ref[s, lb, base:HIST_PAD, :] = new_hist

    cos = cos_ref[...]
    sin = sin_ref[...]
    q0, k0, v0, g0 = (2 * D_CONV + i * D_RET for i in range(4))
    for s in range(nb):
        rows = slice(s * tl, (s + 1) * tl)
        for h in range(N_RET_HEADS):
            hs = slice(h * HEAD_DIM, (h + 1) * HEAD_DIM)
            lane = lambda off: slice(off + h * HEAD_DIM, off + (h + 1) * HEAD_DIM)
            q = p[rows, lane(q0)]
            k = p[rows, lane(k0)]
            v = p[rows, lane(v0)].astype(BF16)
            gate = p[rows, lane(g0)]
            qr = (q * cos + pltpu.roll(q, HEAD_DIM // 2, 1) * sin).astype(BF16)
            kr = (k * cos + pltpu.roll(k, HEAD_DIM // 2, 1) * sin) * (HEAD_DIM ** -0.5)
            state = rstate_ref[s, h]
            g_chunk = math.exp(log_gamma[h] * chunk)
            for ci in range(tl // chunk):
                cs = slice(ci * chunk, (ci + 1) * chunk)
                qc, kc, vc = qr[cs], kr[cs], v[cs]
                sc = lax.dot_general(qc, kc.astype(BF16), (((1,), (1,)), ((), ())),
                                     preferred_element_type=F32) * dmask_ref[h]
                o = (jnp.dot(sc.astype(BF16), vc, preferred_element_type=F32)
                     + jnp.dot(qc, state.astype(BF16), preferred_element_type=F32) * decay_ref[h, 0])
                kz = (kc * decay_ref[h, 1]).astype(BF16)
                state = state * g_chunk + lax.dot_general(
                    kz, vc, (((0,), (0,)), ((), ())), preferred_element_type=F32)
                mu = jnp.mean(o, axis=-1, keepdims=True)
                do = o - mu
                var = jnp.mean(do * do, axis=-1, keepdims=True)
                on = do * lax.rsqrt(var + EPS)
                r = _silu(gate[cs]) * (on * gng_ref[:, hs])
                mix_ref[s * tl + ci * chunk:s * tl + (ci + 1) * chunk,
                        D_CONV + h * HEAD_DIM:D_CONV + (h + 1) * HEAD_DIM] = r.astype(BF16)
            rstate_ref[s, h] = state
            rout_ref[s, h] = state

    out = jnp.dot(mix_ref[...], wout_ref[...], preferred_element_type=F32)
    y_ref[...] = (x + _rmsnorm(out, gpost_ref[...])).reshape(nb, tl, D_MODEL)


def _mixer(x, cos, sin, hist0, r0, gpre, win, cw, cb, lng, lnb, gng, wout, gpost, *, nb, tl):
    bsz, seq, _ = x.shape
    assert bsz % nb == 0 and seq % tl == 0 and tl % V7X_SUBLANES == 0
    chunk = min(RET_CHUNK, tl)
    assert tl % chunk == 0
    m = nb * tl
    body = functools.partial(_mixer_kernel, nb=nb, tl=tl, chunk=chunk)
    weights = (D_MODEL * D_IN + D_MODEL * D_MODEL) * 2
    scratch = (nb * (tl + HIST_PAD) * D_CONV * 4 + nb * N_RET_HEADS * HEAD_DIM * HEAD_DIM * 4
               + m * D_MODEL * 2 + N_RET_HEADS * chunk * (chunk + 2 * HEAD_DIM) * 4)
    return pl.pallas_call(
        body,
        grid=(bsz // nb, seq // tl),
        in_specs=[
            pl.BlockSpec((nb, tl, D_MODEL), lambda b, t: (b, t, 0)),
            pl.BlockSpec((tl, HEAD_DIM), lambda b, t: (t, 0)),
            pl.BlockSpec((tl, HEAD_DIM), lambda b, t: (t, 0)),
            pl.BlockSpec((nb, HIST, D_CONV), lambda b, t: (b, 0, 0)),
            pl.BlockSpec((nb, N_RET_HEADS, HEAD_DIM, HEAD_DIM), lambda b, t: (b, 0, 0, 0)),
            _resident((1, D_MODEL)), _resident((D_MODEL, D_IN)), _resident((CONV_WIDTH, D_CONV)),
            _resident((1, D_CONV)), _resident((1, D_CONV)), _resident((1, D_CONV)),
            _resident((1, D_RET)), _resident((D_MODEL, D_MODEL)), _resident((1, D_MODEL)),
        ],
        out_specs=[
            pl.BlockSpec((nb, tl, D_MODEL), lambda b, t: (b, t, 0)),
            pl.BlockSpec((nb, HIST, D_CONV), lambda b, t: (b, 0, 0)),
            pl.BlockSpec((nb, N_RET_HEADS, HEAD_DIM, HEAD_DIM), lambda b, t: (b, 0, 0, 0)),
        ],
        out_shape=[
            jax.ShapeDtypeStruct((bsz, seq, D_MODEL), F32),
            jax.ShapeDtypeStruct((bsz, HIST, D_CONV), F32),
            jax.ShapeDtypeStruct((bsz, N_RET_HEADS, HEAD_DIM, HEAD_DIM), F32),
        ],
        scratch_shapes=[
            pltpu.VMEM((nb, LANE_BLOCKS, tl + HIST_PAD, V7X_LANES), F32),
            pltpu.VMEM((nb, N_RET_HEADS, HEAD_DIM, HEAD_DIM), F32),
            pltpu.VMEM((m, D_MODEL), BF16),
            pltpu.VMEM((N_RET_HEADS, chunk, chunk), F32),
            pltpu.VMEM((N_RET_HEADS, 2, chunk, HEAD_DIM), F32),
        ],
        compiler_params=pltpu.CompilerParams(
            dimension_semantics=("arbitrary", "arbitrary"),
            vmem_limit_bytes=_vmem_limit(weights + scratch, 2 * m * D_MODEL * 4,
                                         m * (D_IN + 3 * D_MODEL) * 4)),
        name="mixer",
    )(x, cos, sin, hist0, r0, gpre, win, cw, cb, lng, lnb, gng, wout, gpost)


def _rope_tables(pos0, n):
    half = HEAD_DIM // 2
    inv = ROPE_THETA ** (-np.arange(half, dtype=np.float64) / half)
    ang = (pos0 + np.arange(n, dtype=np.float64))[:, None] * inv[None, :]
    cos, sin = np.cos(ang), np.sin(ang)
    return (np.concatenate([cos, cos], axis=-1).astype(np.float32),
            np.concatenate([-sin, sin], axis=-1).astype(np.float32))


def _layer(x, hist0, r0, pos0, w, *, nb, tl):
    bsz, seq, _ = x.shape
    row = lambda a: a.reshape(1, -1)
    x2 = _ffn(x.reshape(bsz * seq, D_MODEL), row(w["ffn1_norm_pre"]), w["ffn1_w_gate"],
              w["ffn1_w_up"], w["ffn1_w_down"], row(w["ffn1_norm_post"]))
    cos, sin = _rope_tables(pos0, seq)
    y, hist, rstate = _mixer(
        x2.reshape(bsz, seq, D_MODEL), cos, sin, hist0, r0, row(w["mix_norm_pre"]), w["w_in"],
        w["conv_w"], row(w["conv_b"]), row(w["conv_ln_g"]), row(w["conv_ln_b"]),
        row(w["ret_gn_g"]), w["w_out"], row(w["mix_norm_post"]), nb=nb, tl=tl)
    y2 = _ffn(y.reshape(bsz * seq, D_MODEL), row(w["ffn2_norm_pre"]), w["ffn2_w_gate"],
              w["ffn2_w_up"], w["ffn2_w_down"], row(w["ffn2_norm_post"]))
    return y2.reshape(bsz, seq, D_MODEL), hist, rstate


_MATMUL_WEIGHTS = ("ffn1_w_gate", "ffn1_w_up", "ffn1_w_down", "w_in", "w_out",
                   "ffn2_w_gate", "ffn2_w_up", "ffn2_w_down")


def kernel(x_prompt, x_sample, state_conv, state_ret, ffn1_norm_pre, ffn1_w_gate, ffn1_w_up, ffn1_w_down, ffn1_norm_post, mix_norm_pre, w_in, conv_w, conv_b, conv_ln_g, conv_ln_b, ret_gn_g, w_out, mix_norm_post, ffn2_norm_pre, ffn2_w_gate, ffn2_w_up, ffn2_w_down, ffn2_norm_post):
    params = dict(
        ffn1_norm_pre=ffn1_norm_pre, ffn1_w_gate=ffn1_w_gate, ffn1_w_up=ffn1_w_up,
        ffn1_w_down=ffn1_w_down, ffn1_norm_post=ffn1_norm_post, mix_norm_pre=mix_norm_pre,
        w_in=w_in, conv_w=conv_w, conv_b=conv_b, conv_ln_g=conv_ln_g, conv_ln_b=conv_ln_b,
        ret_gn_g=ret_gn_g, w_out=w_out, mix_norm_post=mix_norm_post,
        ffn2_norm_pre=ffn2_norm_pre, ffn2_w_gate=ffn2_w_gate, ffn2_w_up=ffn2_w_up,
        ffn2_w_down=ffn2_w_down, ffn2_norm_post=ffn2_norm_post)
    depth = ffn1_norm_pre.shape[0]
    bp, sp, _ = x_prompt.shape
    bs, ss, _ = x_sample.shape
    yp, ys = x_prompt, x_sample
    conv_p, ret_p, conv_s, ret_s = [], [], [], []
    for l in range(depth):
        w = {k: (v[l].astype(BF16) if k in _MATMUL_WEIGHTS else v[l]) for k, v in params.items()}
        hist0 = jnp.zeros((bp, HIST, D_CONV), F32)
        r0 = jnp.zeros((bp, N_RET_HEADS, HEAD_DIM, HEAD_DIM), F32)
        yp, hp, rp = _layer(yp, hist0, r0, 0, w, nb=1, tl=min(MIX_TILE, sp))
        ys, hs, rs = _layer(ys, state_conv[l], state_ret[l], PAST_LEN, w, nb=bs, tl=ss)
        conv_p.append(hp); ret_p.append(rp); conv_s.append(hs); ret_s.append(rs)
    return (yp, ys, jnp.stack(conv_p), jnp.stack(ret_p), jnp.stack(conv_s), jnp.stack(ret_s))
```

```python
import functools
import math

import numpy as np

import jax
import jax.numpy as jnp
from jax import lax
from jax.experimental import pallas as pl
from jax.experimental.pallas import tpu as pltpu

D_MODEL = 1024
D_FF = 2816
D_CONV = 512
D_RET = 512
N_RET_HEADS = 4
HEAD_DIM = 128
CONV_WIDTH = 31
HIST = CONV_WIDTH - 1
ROPE_THETA = 10000.0
EPS = 1e-6
D_IN = 2 * D_CONV + 4 * D_RET
PAST_LEN = 4096

V7X_SUBLANES = 8
V7X_LANES = 128
LANE_BLOCKS = D_CONV // V7X_LANES
V7X_VMEM_BYTES = 64 * 1024 * 1024

HIST_PAD = 32
CONV_ROW_BLOCK = 64
FFN_TILE = 1024
MIX_TILE = 512
RET_CHUNK = 256
WEIGHT_STEPS = 16
V7X_MXU_DIM = 256
FF_SPLITS = (0, 6 * V7X_MXU_DIM, D_FF)
FF_CHUNK = max(b - a for a, b in zip(FF_SPLITS[:-1], FF_SPLITS[1:]))
assert all((b - a) % V7X_MXU_DIM == 0 for a, b in zip(FF_SPLITS[:-1], FF_SPLITS[1:]))

BF16 = jnp.bfloat16
F32 = jnp.float32


def _vmem_limit(resident_bytes, tile_bytes, temp_bytes):
    need = resident_bytes + 2 * tile_bytes + temp_bytes
    return int(min(need * 5 // 4 + (4 << 20), V7X_VMEM_BYTES - (6 << 20)))


def _rmsnorm(x, g):
    ms = jnp.mean(x * x, axis=-1, keepdims=True)
    return x * lax.rsqrt(ms + EPS) * g


def _lanes(block):
    return slice(block * V7X_LANES, (block + 1) * V7X_LANES)


def _silu(x):
    return x * jax.nn.sigmoid(x)


def _resident(shape):
    nd = len(shape)
    return pl.BlockSpec(shape, lambda *_: (0,) * nd, pipeline_mode=pl.Buffered(1))


def _ffn_math(x, gpre_ref, wg_ref, wu_ref, wd_ref, gpost_ref):
    hb = _rmsnorm(x, gpre_ref[...]).astype(BF16)
    acc = None
    for lo, hi in zip(FF_SPLITS[:-1], FF_SPLITS[1:]):
        sl = slice(lo, hi)
        g = jnp.dot(hb, wg_ref[:, sl], preferred_element_type=F32)
        u = jnp.dot(hb, wu_ref[:, sl], preferred_element_type=F32)
        a = (_silu(g) * u).astype(BF16)
        d = jnp.dot(a, wd_ref[sl, :], preferred_element_type=F32)
        acc = d if acc is None else acc + d
    return x + 0.5 * _rmsnorm(acc, gpost_ref[...])


def _ffn_kernel(x_ref, gpre_ref, wg_ref, wu_ref, wd_ref, gpost_ref, o_ref):
    o_ref[...] = _ffn_math(x_ref[...], gpre_ref, wg_ref, wu_ref, wd_ref, gpost_ref)


def _cast_rows(step, src_ref, out_ref, resident_ref):
    rows = src_ref.shape[0]
    block = src_ref[...].astype(BF16)
    out_ref[...] = block
    resident_ref[pl.ds(pl.multiple_of(step * rows, rows), rows), :] = block


def _ffn_cast_kernel(x_ref, gpre_ref, wg32_ref, wu32_ref, wd32_ref, gpost_ref,
                     o_ref, wg16_ref, wu16_ref, wd16_ref, wg_ref, wu_ref, wd_ref):
    step = pl.program_id(0)

    @pl.when(step < WEIGHT_STEPS)
    def _():
        _cast_rows(step, wg32_ref, wg16_ref, wg_ref)
        _cast_rows(step, wu32_ref, wu16_ref, wu_ref)
        _cast_rows(step, wd32_ref, wd16_ref, wd_ref)

    @pl.when(step >= WEIGHT_STEPS)
    def _():
        o_ref[...] = _ffn_math(x_ref[...], gpre_ref, wg_ref, wu_ref, wd_ref, gpost_ref)


FFN_WEIGHT_BYTES = 3 * D_MODEL * D_FF * 2


def _ffn_temp_bytes(tm):
    return tm * (3 * FF_CHUNK + 2 * D_MODEL) * 4


def _ffn(x2d, gpre, wg, wu, wd, gpost):
    rows = x2d.shape[0]
    tm = min(FFN_TILE, rows)
    assert rows % tm == 0
    tile = pl.BlockSpec((tm, D_MODEL), lambda i: (i, 0))
    return pl.pallas_call(
        _ffn_kernel,
        grid=(rows // tm,),
        in_specs=[tile, _resident((1, D_MODEL)), _resident((D_MODEL, D_FF)),
                  _resident((D_MODEL, D_FF)), _resident((D_FF, D_MODEL)),
                  _resident((1, D_MODEL))],
        out_specs=tile,
        out_shape=jax.ShapeDtypeStruct((rows, D_MODEL), F32),
        compiler_params=pltpu.CompilerParams(
            dimension_semantics=("arbitrary",),
            vmem_limit_bytes=_vmem_limit(FFN_WEIGHT_BYTES, 2 * tm * D_MODEL * 4,
                                         _ffn_temp_bytes(tm))),
        name="ffn",
    )(x2d, gpre, wg, wu, wd, gpost)


def _row_blocks(shape, dtype_bytes):
    rows, cols = shape
    assert rows % WEIGHT_STEPS == 0
    spec = pl.BlockSpec((rows // WEIGHT_STEPS, cols),
                        lambda i: (jnp.minimum(i, WEIGHT_STEPS - 1), 0))
    return spec, 2 * (rows // WEIGHT_STEPS) * cols * dtype_bytes


def _ffn_cast(x2d, gpre, wg32, wu32, wd32, gpost):
    rows = x2d.shape[0]
    tm = min(FFN_TILE, rows)
    assert rows % tm == 0
    tile = pl.BlockSpec((tm, D_MODEL), lambda i: (jnp.maximum(i - WEIGHT_STEPS, 0), 0))
    shapes = [(D_MODEL, D_FF), (D_MODEL, D_FF), (D_FF, D_MODEL)]
    in_blocks, in_bytes = zip(*[_row_blocks(s, 4) for s in shapes])
    out_blocks, out_bytes = zip(*[_row_blocks(s, 2) for s in shapes])
    return pl.pallas_call(
        _ffn_cast_kernel,
        grid=(WEIGHT_STEPS + rows // tm,),
        in_specs=[tile, _resident((1, D_MODEL)), *in_blocks, _resident((1, D_MODEL))],
        out_specs=[tile, *out_blocks],
        out_shape=[jax.ShapeDtypeStruct((rows, D_MODEL), F32)]
                  + [jax.ShapeDtypeStruct(s, BF16) for s in shapes],
        scratch_shapes=[pltpu.VMEM(s, BF16) for s in shapes],
        compiler_params=pltpu.CompilerParams(
            dimension_semantics=("arbitrary",),
            vmem_limit_bytes=_vmem_limit(FFN_WEIGHT_BYTES + sum(in_bytes) + sum(out_bytes),
                                         2 * tm * D_MODEL * 4, _ffn_temp_bytes(tm))),
        name="ffn_cast",
    )(x2d, gpre, wg32, wu32, wd32, gpost)


def _mixer_kernel(x_ref, cos_ref, sin_ref, hist0_ref, r0_ref, gpre_ref, win_ref, cw_ref,
                  cb_ref, lng_ref, lnb_ref, gng_ref, wout_ref, gpost_ref,
                  y_ref, hist_ref, rout_ref,
                  uext_ref, rstate_ref, mix_ref, dmask_ref, decay_ref, *, nb, tl, chunk):
    t = pl.program_id(1)
    log_gamma = [math.log1p(-(2.0 ** (-5.0 - h))) for h in range(N_RET_HEADS)]

    @pl.when(t == 0)
    def _init():
        for lb in range(LANE_BLOCKS):
            uext_ref[:, lb, HIST_PAD - HIST:HIST_PAD, :] = hist0_ref[:, :, _lanes(lb)]
        rstate_ref[...] = r0_ref[...]
        ii = lax.broadcasted_iota(jnp.int32, (chunk, chunk), 0)
        jj = lax.broadcasted_iota(jnp.int32, (chunk, chunk), 1)
        rel = (ii - jj).astype(F32)
        row = lax.broadcasted_iota(jnp.int32, (chunk, HEAD_DIM), 0).astype(F32)
        for h in range(N_RET_HEADS):
            dmask_ref[h] = jnp.where(rel >= 0.0, jnp.exp(log_gamma[h] * jnp.maximum(rel, 0.0)), 0.0)
            decay_ref[h, 0] = jnp.exp(log_gamma[h] * (row + 1.0))
            decay_ref[h, 1] = jnp.exp(log_gamma[h] * (chunk - 1.0 - row))

    m = nb * tl
    x = x_ref[...].reshape(m, D_MODEL)
    hb = _rmsnorm(x, gpre_ref[...]).astype(BF16)
    ab = jnp.dot(hb, win_ref[:, :2 * D_CONV], preferred_element_type=F32)
    p = jnp.dot(hb, win_ref[:, 2 * D_CONV:], preferred_element_type=F32)

    u = ab[:, :D_CONV] * jax.nn.sigmoid(ab[:, D_CONV:])
    for s in range(nb):
        for lb in range(LANE_BLOCKS):
            uext_ref[s, lb, HIST_PAD:HIST_PAD + tl, :] = u[s * tl:(s + 1) * tl, _lanes(lb)]
    rb = min(CONV_ROW_BLOCK, tl)
    base = HIST_PAD - HIST
    for s in range(nb):
        for r0 in range(0, tl, rb):
            taps = []
            for lb in range(LANE_BLOCKS):
                acc = jnp.broadcast_to(cb_ref[:, _lanes(lb)], (rb, V7X_LANES))
                for j in range(CONV_WIDTH):
                    acc = acc + (cw_ref[j:j + 1, _lanes(lb)]
                                 * uext_ref[s, lb, r0 + base + j:r0 + base + j + rb, :])
                taps.append(acc)
            acc = jnp.concatenate(taps, axis=-1)
            mu = jnp.mean(acc, axis=-1, keepdims=True)
            dc = acc - mu
            var = jnp.mean(dc * dc, axis=-1, keepdims=True)
            c = _silu(dc * lax.rsqrt(var + EPS) * lng_ref[...] + lnb_ref[...])
            mix_ref[s * tl + r0:s * tl + r0 + rb, 0:D_CONV] = c.astype(BF16)
        for lb in range(LANE_BLOCKS):
            new_hist = uext_ref[s, lb, tl + base:tl + HIST_PAD, :]
            hist_ref[s, :, _lanes(lb)] = new_hist
            uext_ref[s, lb, base:HIST_PAD, :] = new_hist

    cos = cos_ref[...]
    sin = sin_ref[...]
    q0, k0, v0, g0 = (i * D_RET for i in range(4))
    for s in range(nb):
        rows = slice(s * tl, (s + 1) * tl)
        for h in range(N_RET_HEADS):
            hs = slice(h * HEAD_DIM, (h + 1) * HEAD_DIM)
            lane = lambda off: slice(off + h * HEAD_DIM, off + (h + 1) * HEAD_DIM)
            q = p[rows, lane(q0)]
            k = p[rows, lane(k0)]
            v = p[rows, lane(v0)].astype(BF16)
            gate = p[rows, lane(g0)]
            qr = (q * cos + pltpu.roll(q, HEAD_DIM // 2, 1) * sin).astype(BF16)
            kr = (k * cos + pltpu.roll(k, HEAD_DIM // 2, 1) * sin) * (HEAD_DIM ** -0.5)
            state = rstate_ref[s, h]
            g_chunk = math.exp(log_gamma[h] * chunk)
            for ci in range(tl // chunk):
                cs = slice(ci * chunk, (ci + 1) * chunk)
                qc, kc, vc = qr[cs], kr[cs], v[cs]
                sc = lax.dot_general(qc, kc.astype(BF16), (((1,), (1,)), ((), ())),
                                     preferred_element_type=F32) * dmask_ref[h]
                o = (jnp.dot(sc.astype(BF16), vc, preferred_element_type=F32)
                     + jnp.dot(qc, state.astype(BF16), preferred_element_type=F32) * decay_ref[h, 0])
                kz = (kc * decay_ref[h, 1]).astype(BF16)
                state = state * g_chunk + lax.dot_general(
                    kz, vc, (((0,), (0,)), ((), ())), preferred_element_type=F32)
                mu = jnp.mean(o, axis=-1, keepdims=True)
                do = o - mu
                var = jnp.mean(do * do, axis=-1, keepdims=True)
                on = do * lax.rsqrt(var + EPS)
                r = _silu(gate[cs]) * (on * gng_ref[:, hs])
                mix_ref[s * tl + ci * chunk:s * tl + (ci + 1) * chunk,
                        D_CONV + h * HEAD_DIM:D_CONV + (h + 1) * HEAD_DIM] = r.astype(BF16)
            rstate_ref[s, h] = state
            rout_ref[s, h] = state

    out = jnp.dot(mix_ref[...], wout_ref[...], preferred_element_type=F32)
    y_ref[...] = (x + _rmsnorm(out, gpost_ref[...])).reshape(nb, tl, D_MODEL)


def _mixer(x, cos, sin, hist0, r0, gpre, win, cw, cb, lng, lnb, gng, wout, gpost, *, nb, tl):
    bsz, seq, _ = x.shape
    assert bsz % nb == 0 and seq % tl == 0 and tl % V7X_SUBLANES == 0
    chunk = min(RET_CHUNK, tl)
    assert tl % chunk == 0
    m = nb * tl
    body = functools.partial(_mixer_kernel, nb=nb, tl=tl, chunk=chunk)
    weights = (D_MODEL * D_IN + D_MODEL * D_MODEL) * 2
    scratch = (nb * (tl + HIST_PAD) * D_CONV * 4 + nb * N_RET_HEADS * HEAD_DIM * HEAD_DIM * 4
               + m * D_MODEL * 2 + N_RET_HEADS * chunk * (chunk + 2 * HEAD_DIM) * 4)
    return pl.pallas_call(
        body,
        grid=(bsz // nb, seq // tl),
        in_specs=[
            pl.BlockSpec((nb, tl, D_MODEL), lambda b, t: (b, t, 0)),
            pl.BlockSpec((tl, HEAD_DIM), lambda b, t: (t, 0)),
            pl.BlockSpec((tl, HEAD_DIM), lambda b, t: (t, 0)),
            pl.BlockSpec((nb, HIST, D_CONV), lambda b, t: (b, 0, 0)),
            pl.BlockSpec((nb, N_RET_HEADS, HEAD_DIM, HEAD_DIM), lambda b, t: (b, 0, 0, 0)),
            _resident((1, D_MODEL)), _resident((D_MODEL, D_IN)), _resident((CONV_WIDTH, D_CONV)),
            _resident((1, D_CONV)), _resident((1, D_CONV)), _resident((1, D_CONV)),
            _resident((1, D_RET)), _resident((D_MODEL, D_MODEL)), _resident((1, D_MODEL)),
        ],
        out_specs=[
            pl.BlockSpec((nb, tl, D_MODEL), lambda b, t: (b, t, 0)),
            pl.BlockSpec((nb, HIST, D_CONV), lambda b, t: (b, 0, 0)),
            pl.BlockSpec((nb, N_RET_HEADS, HEAD_DIM, HEAD_DIM), lambda b, t: (b, 0, 0, 0)),
        ],
        out_shape=[
            jax.ShapeDtypeStruct((bsz, seq, D_MODEL), F32),
            jax.ShapeDtypeStruct((bsz, HIST, D_CONV), F32),
            jax.ShapeDtypeStruct((bsz, N_RET_HEADS, HEAD_DIM, HEAD_DIM), F32),
        ],
        scratch_shapes=[
            pltpu.VMEM((nb, LANE_BLOCKS, tl + HIST_PAD, V7X_LANES), F32),
            pltpu.VMEM((nb, N_RET_HEADS, HEAD_DIM, HEAD_DIM), F32),
            pltpu.VMEM((m, D_MODEL), BF16),
            pltpu.VMEM((N_RET_HEADS, chunk, chunk), F32),
            pltpu.VMEM((N_RET_HEADS, 2, chunk, HEAD_DIM), F32),
        ],
        compiler_params=pltpu.CompilerParams(
            dimension_semantics=("arbitrary", "arbitrary"),
            vmem_limit_bytes=_vmem_limit(weights + scratch, 2 * m * D_MODEL * 4,
                                         m * (D_IN + 3 * D_MODEL) * 4)),
        name="mixer",
    )(x, cos, sin, hist0, r0, gpre, win, cw, cb, lng, lnb, gng, wout, gpost)


def _rope_tables(pos0, n):
    half = HEAD_DIM // 2
    inv = ROPE_THETA ** (-np.arange(half, dtype=np.float64) / half)
    ang = (pos0 + np.arange(n, dtype=np.float64))[:, None] * inv[None, :]
    cos, sin = np.cos(ang), np.sin(ang)
    return (np.concatenate([cos, cos], axis=-1).astype(np.float32),
            np.concatenate([-sin, sin], axis=-1).astype(np.float32))


def _layer(x, hist0, r0, pos0, ffn1_w, mix_w, ffn2_w, *, nb, tl, cast):
    bsz, seq, _ = x.shape
    ffn = _ffn_cast if cast else _ffn
    flat = lambda a: a.reshape(bsz * seq, D_MODEL)
    x, *ffn1_bf16 = _as_list(ffn(flat(x), *ffn1_w))
    cos, sin = _rope_tables(pos0, seq)
    y, hist, rstate = _mixer(x.reshape(bsz, seq, D_MODEL), cos, sin, hist0, r0, *mix_w,
                             nb=nb, tl=tl)
    y, *ffn2_bf16 = _as_list(ffn(flat(y), *ffn2_w))
    return y.reshape(bsz, seq, D_MODEL), hist, rstate, ffn1_bf16, ffn2_bf16


def _as_list(out):
    return list(out) if isinstance(out, (list, tuple)) else [out]


def kernel(x_prompt, x_sample, state_conv, state_ret, ffn1_norm_pre, ffn1_w_gate, ffn1_w_up, ffn1_w_down, ffn1_norm_post, mix_norm_pre, w_in, conv_w, conv_b, conv_ln_g, conv_ln_b, ret_gn_g, w_out, mix_norm_post, ffn2_norm_pre, ffn2_w_gate, ffn2_w_up, ffn2_w_down, ffn2_norm_post):
    depth = ffn1_norm_pre.shape[0]
    bp, sp, _ = x_prompt.shape
    bs, ss, _ = x_sample.shape
    row = lambda a: a.reshape(1, -1)
    yp, ys = x_prompt, x_sample
    conv_p, ret_p, conv_s, ret_s = [], [], [], []
    for l in range(depth):
        ffn1_w = (row(ffn1_norm_pre[l]), ffn1_w_gate[l], ffn1_w_up[l], ffn1_w_down[l],
                  row(ffn1_norm_post[l]))
        ffn2_w = (row(ffn2_norm_pre[l]), ffn2_w_gate[l], ffn2_w_up[l], ffn2_w_down[l],
                  row(ffn2_norm_post[l]))
        mix_w = (row(mix_norm_pre[l]), w_in[l].astype(BF16), conv_w[l], row(conv_b[l]),
                 row(conv_ln_g[l]), row(conv_ln_b[l]), row(ret_gn_g[l]), w_out[l].astype(BF16),
                 row(mix_norm_post[l]))
        hist0 = jnp.zeros((bp, HIST, D_CONV), F32)
        r0 = jnp.zeros((bp, N_RET_HEADS, HEAD_DIM, HEAD_DIM), F32)
        yp, hp, rp, ffn1_bf16, ffn2_bf16 = _layer(
            yp, hist0, r0, 0, ffn1_w, mix_w, ffn2_w, nb=bp, tl=min(MIX_TILE, sp), cast=True)
        ffn1_w = (ffn1_w[0], *ffn1_bf16, ffn1_w[4])
        ffn2_w = (ffn2_w[0], *ffn2_bf16, ffn2_w[4])
        ys, hs, rs, _, _ = _layer(ys, state_conv[l], state_ret[l], PAST_LEN, ffn1_w, mix_w, ffn2_w,
                                  nb=bs, tl=ss, cast=False)
        conv_p.append(hp); ret_p.append(rp); conv_s.append(hs); ret_s.append(rs)
    return (yp, ys, jnp.stack(conv_p), jnp.stack(ret_p), jnp.stack(conv_s), jnp.stack(ret_s))
```

```python
import functools
import math

import numpy as np

import jax
import jax.numpy as jnp
from jax import lax
from jax.experimental import pallas as pl
from jax.experimental.pallas import tpu as pltpu

D_MODEL = 1024
D_FF = 2816
D_CONV = 512
D_RET = 512
N_RET_HEADS = 4
HEAD_DIM = 128
CONV_WIDTH = 31
HIST = CONV_WIDTH - 1
ROPE_THETA = 10000.0
EPS = 1e-6
D_IN = 2 * D_CONV + 4 * D_RET
PAST_LEN = 4096

V7X_SUBLANES = 8
V7X_LANES = 128
LANE_BLOCKS = D_CONV // V7X_LANES
V7X_VMEM_BYTES = 64 * 1024 * 1024

HIST_PAD = 32
CONV_ROW_BLOCK = 64
FFN_TILE = 1024
MIX_TILE = 512
RET_CHUNK = 256
V7X_MXU_DIM = 256
FF_CHUNKS = D_FF // V7X_MXU_DIM
assert FF_CHUNKS * V7X_MXU_DIM == D_FF

BF16 = jnp.bfloat16
F32 = jnp.float32


def _vmem_limit(resident_bytes, tile_bytes, temp_bytes):
    need = resident_bytes + 2 * tile_bytes + temp_bytes
    return int(min(need * 5 // 4 + (4 << 20), V7X_VMEM_BYTES - (6 << 20)))


def _rmsnorm(x, g):
    ms = jnp.mean(x * x, axis=-1, keepdims=True)
    return x * lax.rsqrt(ms + EPS) * g


def _lanes(block):
    return slice(block * V7X_LANES, (block + 1) * V7X_LANES)


def _silu(x):
    return x * jax.nn.sigmoid(x)


def _resident(shape):
    nd = len(shape)
    return pl.BlockSpec(shape, lambda *_: (0,) * nd, pipeline_mode=pl.Buffered(1))


def _ffn_chunk(hb, wg_ref, wu_ref, wd_ref, c):
    g = jnp.dot(hb, wg_ref[c], preferred_element_type=F32)
    u = jnp.dot(hb, wu_ref[c], preferred_element_type=F32)
    a = (_silu(g) * u).astype(BF16)
    return jnp.dot(a, wd_ref[c], preferred_element_type=F32)


def _ffn_math(x, gpre_ref, wg_ref, wu_ref, wd_ref, gpost_ref):
    hb = _rmsnorm(x, gpre_ref[...]).astype(BF16)
    acc = _ffn_chunk(hb, wg_ref, wu_ref, wd_ref, 0)
    for c in range(1, FF_CHUNKS):
        acc = acc + _ffn_chunk(hb, wg_ref, wu_ref, wd_ref, c)
    return x + 0.5 * _rmsnorm(acc, gpost_ref[...])


def _ffn_kernel(xp_ref, xs_ref, gpre_ref, wg32_ref, wu32_ref, wd32_ref, gpost_ref,
                yp_ref, ys_ref, wg_ref, wu_ref, wd_ref, hb_ref, acc_ref, *, n_tiles):
    step = pl.program_id(0)
    last = FF_CHUNKS - 1

    @pl.when(step <= last)
    def _():
        @pl.when(step == 0)
        def _():
            hb_ref[...] = _rmsnorm(xp_ref[...], gpre_ref[...]).astype(BF16)

        wg_ref[step] = wg32_ref[...].astype(BF16)
        wu_ref[step] = wu32_ref[...].astype(BF16)
        wd_ref[step] = wd32_ref[...].astype(BF16)
        d = _ffn_chunk(hb_ref[...], wg_ref, wu_ref, wd_ref, step)

        @pl.when(step == 0)
        def _():
            acc_ref[...] = d

        @pl.when(step > 0)
        def _():
            acc_ref[...] += d

        @pl.when(step == last)
        def _():
            yp_ref[...] = xp_ref[...] + 0.5 * _rmsnorm(acc_ref[...], gpost_ref[...])

    @pl.when(jnp.logical_and(step > last, step < last + n_tiles))
    def _():
        yp_ref[...] = _ffn_math(xp_ref[...], gpre_ref, wg_ref, wu_ref, wd_ref, gpost_ref)

    @pl.when(step == last + n_tiles)
    def _():
        ys_ref[...] = _ffn_math(xs_ref[...], gpre_ref, wg_ref, wu_ref, wd_ref, gpost_ref)


def _ffn(xp2d, xs2d, gpre, wg32, wu32, wd32, gpost):
    rows, sample_rows = xp2d.shape[0], xs2d.shape[0]
    tm = FFN_TILE
    assert rows % tm == 0 and sample_rows % V7X_SUBLANES == 0
    n_tiles = rows // tm
    last = FF_CHUNKS - 1
    tile_index = lambda s: (jnp.clip(s - last, 0, n_tiles - 1), 0)
    chunk_index = lambda s: jnp.minimum(s, last)
    prompt_tile = pl.BlockSpec((tm, D_MODEL), tile_index)
    sample_tile = pl.BlockSpec((sample_rows, D_MODEL), lambda s: (0, 0))
    w_bytes = 3 * D_MODEL * D_FF
    return pl.pallas_call(
        functools.partial(_ffn_kernel, n_tiles=n_tiles),
        grid=(last + n_tiles + 1,),
        in_specs=[prompt_tile, sample_tile, _resident((1, D_MODEL)),
                  pl.BlockSpec((D_MODEL, V7X_MXU_DIM), lambda s: (0, chunk_index(s))),
                  pl.BlockSpec((D_MODEL, V7X_MXU_DIM), lambda s: (0, chunk_index(s))),
                  pl.BlockSpec((V7X_MXU_DIM, D_MODEL), lambda s: (chunk_index(s), 0)),
                  _resident((1, D_MODEL))],
        out_specs=[prompt_tile, sample_tile],
        out_shape=[jax.ShapeDtypeStruct((rows, D_MODEL), F32),
                   jax.ShapeDtypeStruct((sample_rows, D_MODEL), F32)],
        scratch_shapes=[pltpu.VMEM((FF_CHUNKS, D_MODEL, V7X_MXU_DIM), BF16),
                        pltpu.VMEM((FF_CHUNKS, D_MODEL, V7X_MXU_DIM), BF16),
                        pltpu.VMEM((FF_CHUNKS, V7X_MXU_DIM, D_MODEL), BF16),
                        pltpu.VMEM((tm, D_MODEL), BF16),
                        pltpu.VMEM((tm, D_MODEL), F32)],
        compiler_params=pltpu.CompilerParams(
            dimension_semantics=("arbitrary",),
            vmem_limit_bytes=_vmem_limit(
                2 * w_bytes + 2 * 3 * D_MODEL * V7X_MXU_DIM * 4 + tm * D_MODEL * 6,
                2 * (tm + sample_rows) * D_MODEL * 4,
                tm * (3 * V7X_MXU_DIM + 3 * D_MODEL) * 4)),
        name="ffn",
    )(xp2d, xs2d, gpre, wg32, wu32, wd32, gpost)


def _mixer_kernel(x_ref, cos_ref, sin_ref, hist0_ref, r0_ref, gpre_ref, win_ref, cw_ref,
                  cb_ref, lng_ref, lnb_ref, gng_ref, wout_ref, gpost_ref,
                  y_ref, hist_ref, rout_ref,
                  uext_ref, rstate_ref, mix_ref, dmask_ref, decay_ref, *, nb, tl, chunk):
    t = pl.program_id(1)
    log_gamma = [math.log1p(-(2.0 ** (-5.0 - h))) for h in range(N_RET_HEADS)]

    @pl.when(t == 0)
    def _init():
        for lb in range(LANE_BLOCKS):
            uext_ref[:, lb, HIST_PAD - HIST:HIST_PAD, :] = hist0_ref[:, :, _lanes(lb)]
        rstate_ref[...] = r0_ref[...]
        ii = lax.broadcasted_iota(jnp.int32, (chunk, chunk), 0)
        jj = lax.broadcasted_iota(jnp.int32, (chunk, chunk), 1)
        rel = (ii - jj).astype(F32)
        row = lax.broadcasted_iota(jnp.int32, (chunk, HEAD_DIM), 0).astype(F32)
        for h in range(N_RET_HEADS):
            dmask_ref[h] = jnp.where(rel >= 0.0, jnp.exp(log_gamma[h] * jnp.maximum(rel, 0.0)), 0.0)
            decay_ref[h, 0] = jnp.exp(log_gamma[h] * (row + 1.0))
            decay_ref[h, 1] = jnp.exp(log_gamma[h] * (chunk - 1.0 - row))

    m = nb * tl
    x = x_ref[...].reshape(m, D_MODEL)
    hb = _rmsnorm(x, gpre_ref[...]).astype(BF16)
    ab = jnp.dot(hb, win_ref[:, :2 * D_CONV], preferred_element_type=F32)
    p = jnp.dot(hb, win_ref[:, 2 * D_CONV:], preferred_element_type=F32)

    u = ab[:, :D_CONV] * jax.nn.sigmoid(ab[:, D_CONV:])
    for s in range(nb):
        for lb in range(LANE_BLOCKS):
            uext_ref[s, lb, HIST_PAD:HIST_PAD + tl, :] = u[s * tl:(s + 1) * tl, _lanes(lb)]
    rb = min(CONV_ROW_BLOCK, tl)
    base = HIST_PAD - HIST
    for s in range(nb):
        for r0 in range(0, tl, rb):
            taps = []
            for lb in range(LANE_BLOCKS):
                acc = jnp.broadcast_to(cb_ref[:, _lanes(lb)], (rb, V7X_LANES))
                for j in range(CONV_WIDTH):
                    acc = acc + (cw_ref[j:j + 1, _lanes(lb)]
                                 * uext_ref[s, lb, r0 + base + j:r0 + base + j + rb, :])
                taps.append(acc)
            acc = jnp.concatenate(taps, axis=-1)
            mu = jnp.mean(acc, axis=-1, keepdims=True)
            dc = acc - mu
            var = jnp.mean(dc * dc, axis=-1, keepdims=True)
            c = _silu(dc * lax.rsqrt(var + EPS) * lng_ref[...] + lnb_ref[...])
            mix_ref[s * tl + r0:s * tl + r0 + rb, 0:D_CONV] = c.astype(BF16)
        for lb in range(LANE_BLOCKS):
            new_hist = uext_ref[s, lb, tl + base:tl + HIST_PAD, :]
            hist_ref[s, :, _lanes(lb)] = new_hist
            uext_ref[s, lb, base:HIST_PAD, :] = new_hist

    cos = cos_ref[...]
    sin = sin_ref[...]
    q0, k0, v0, g0 = (i * D_RET for i in range(4))
    for s in range(nb):
        rows = slice(s * tl, (s + 1) * tl)
        for h in range(N_RET_HEADS):
            hs = slice(h * HEAD_DIM, (h + 1) * HEAD_DIM)
            lane = lambda off: slice(off + h * HEAD_DIM, off + (h + 1) * HEAD_DIM)
            q = p[rows, lane(q0)]
            k = p[rows, lane(k0)]
            v = p[rows, lane(v0)].astype(BF16)
            gate = p[rows, lane(g0)]
            qr = (q * cos + pltpu.roll(q, HEAD_DIM // 2, 1) * sin).astype(BF16)
            kr = (k * cos + pltpu.roll(k, HEAD_DIM // 2, 1) * sin) * (HEAD_DIM ** -0.5)
            state = rstate_ref[s, h]
            g_chunk = math.exp(log_gamma[h] * chunk)
            for ci in range(tl // chunk):
                cs = slice(ci * chunk, (ci + 1) * chunk)
                qc, kc, vc = qr[cs], kr[cs], v[cs]
                sc = lax.dot_general(qc, kc.astype(BF16), (((1,), (1,)), ((), ())),
                                     preferred_element_type=F32) * dmask_ref[h]
                o = (jnp.dot(sc.astype(BF16), vc, preferred_element_type=F32)
                     + jnp.dot(qc, state.astype(BF16), preferred_element_type=F32) * decay_ref[h, 0])
                kz = (kc * decay_ref[h, 1]).astype(BF16)
                state = state * g_chunk + lax.dot_general(
                    kz, vc, (((0,), (0,)), ((), ())), preferred_element_type=F32)
                mu = jnp.mean(o, axis=-1, keepdims=True)
                do = o - mu
                var = jnp.mean(do * do, axis=-1, keepdims=True)
                on = do * lax.rsqrt(var + EPS)
                r = _silu(gate[cs]) * (on * gng_ref[:, hs])
                mix_ref[s * tl + ci * chunk:s * tl + (ci + 1) * chunk,
                        D_CONV + h * HEAD_DIM:D_CONV + (h + 1) * HEAD_DIM] = r.astype(BF16)
            rstate_ref[s, h] = state
            rout_ref[s, h] = state

    out = jnp.dot(mix_ref[...], wout_ref[...], preferred_element_type=F32)
    y_ref[...] = (x + _rmsnorm(out, gpost_ref[...])).reshape(nb, tl, D_MODEL)


def _mixer(x, cos, sin, hist0, r0, gpre, win, cw, cb, lng, lnb, gng, wout, gpost, *, nb, tl):
    bsz, seq, _ = x.shape
    assert bsz % nb == 0 and seq % tl == 0 and tl % V7X_SUBLANES == 0
    chunk = min(RET_CHUNK, tl)
    assert tl % chunk == 0
    m = nb * tl
    body = functools.partial(_mixer_kernel, nb=nb, tl=tl, chunk=chunk)
    weights = (D_MODEL * D_IN + D_MODEL * D_MODEL) * 2
    scratch = (nb * (tl + HIST_PAD) * D_CONV * 4 + nb * N_RET_HEADS * HEAD_DIM * HEAD_DIM * 4
               + m * D_MODEL * 2 + N_RET_HEADS * chunk * (chunk + 2 * HEAD_DIM) * 4)
    return pl.pallas_call(
        body,
        grid=(bsz // nb, seq // tl),
        in_specs=[
            pl.BlockSpec((nb, tl, D_MODEL), lambda b, t: (b, t, 0)),
            pl.BlockSpec((tl, HEAD_DIM), lambda b, t: (t, 0)),
            pl.BlockSpec((tl, HEAD_DIM), lambda b, t: (t, 0)),
            pl.BlockSpec((nb, HIST, D_CONV), lambda b, t: (b, 0, 0)),
            pl.BlockSpec((nb, N_RET_HEADS, HEAD_DIM, HEAD_DIM), lambda b, t: (b, 0, 0, 0)),
            _resident((1, D_MODEL)), _resident((D_MODEL, D_IN)), _resident((CONV_WIDTH, D_CONV)),
            _resident((1, D_CONV)), _resident((1, D_CONV)), _resident((1, D_CONV)),
            _resident((1, D_RET)), _resident((D_MODEL, D_MODEL)), _resident((1, D_MODEL)),
        ],
        out_specs=[
            pl.BlockSpec((nb, tl, D_MODEL), lambda b, t: (b, t, 0)),
            pl.BlockSpec((nb, HIST, D_CONV), lambda b, t: (b, 0, 0)),
            pl.BlockSpec((nb, N_RET_HEADS, HEAD_DIM, HEAD_DIM), lambda b, t: (b, 0, 0, 0)),
        ],
        out_shape=[
            jax.ShapeDtypeStruct((bsz, seq, D_MODEL), F32),
            jax.ShapeDtypeStruct((bsz, HIST, D_CONV), F32),
            jax.ShapeDtypeStruct((bsz, N_RET_HEADS, HEAD_DIM, HEAD_DIM), F32),
        ],
        scratch_shapes=[
            pltpu.VMEM((nb, LANE_BLOCKS, tl + HIST_PAD, V7X_LANES), F32),
            pltpu.VMEM((nb, N_RET_HEADS, HEAD_DIM, HEAD_DIM), F32),
            pltpu.VMEM((m, D_MODEL), BF16),
            pltpu.VMEM((N_RET_HEADS, chunk, chunk), F32),
            pltpu.VMEM((N_RET_HEADS, 2, chunk, HEAD_DIM), F32),
        ],
        compiler_params=pltpu.CompilerParams(
            dimension_semantics=("arbitrary", "arbitrary"),
            vmem_limit_bytes=_vmem_limit(weights + scratch, 2 * m * D_MODEL * 4,
                                         m * (D_IN + 3 * D_MODEL) * 4)),
        name="mixer",
    )(x, cos, sin, hist0, r0, gpre, win, cw, cb, lng, lnb, gng, wout, gpost)


def _rope_tables(pos0, n):
    half = HEAD_DIM // 2
    inv = ROPE_THETA ** (-np.arange(half, dtype=np.float64) / half)
    ang = (pos0 + np.arange(n, dtype=np.float64))[:, None] * inv[None, :]
    cos, sin = np.cos(ang), np.sin(ang)
    return (np.concatenate([cos, cos], axis=-1).astype(np.float32),
            np.concatenate([-sin, sin], axis=-1).astype(np.float32))


def _ffn_both(xp, xs, ffn_w):
    yp, ys = _ffn(xp.reshape(-1, D_MODEL), xs.reshape(-1, D_MODEL), *ffn_w)
    return yp.reshape(xp.shape), ys.reshape(xs.shape)


def kernel(x_prompt, x_sample, state_conv, state_ret, ffn1_norm_pre, ffn1_w_gate, ffn1_w_up, ffn1_w_down, ffn1_norm_post, mix_norm_pre, w_in, conv_w, conv_b, conv_ln_g, conv_ln_b, ret_gn_g, w_out, mix_norm_post, ffn2_norm_pre, ffn2_w_gate, ffn2_w_up, ffn2_w_down, ffn2_norm_post):
    depth = ffn1_norm_pre.shape[0]
    bp, sp, _ = x_prompt.shape
    bs, ss, _ = x_sample.shape
    row = lambda a: a.reshape(1, -1)
    yp, ys = x_prompt, x_sample
    conv_p, ret_p, conv_s, ret_s = [], [], [], []
    for l in range(depth):
        ffn1_w = (row(ffn1_norm_pre[l]), ffn1_w_gate[l], ffn1_w_up[l], ffn1_w_down[l],
                  row(ffn1_norm_post[l]))
        ffn2_w = (row(ffn2_norm_pre[l]), ffn2_w_gate[l], ffn2_w_up[l], ffn2_w_down[l],
                  row(ffn2_norm_post[l]))
        mix_w = (row(mix_norm_pre[l]), w_in[l].astype(BF16), conv_w[l], row(conv_b[l]),
                 row(conv_ln_g[l]), row(conv_ln_b[l]), row(ret_gn_g[l]), w_out[l].astype(BF16),
                 row(mix_norm_post[l]))
        hist0 = jnp.zeros((bp, HIST, D_CONV), F32)
        r0 = jnp.zeros((bp, N_RET_HEADS, HEAD_DIM, HEAD_DIM), F32)
        yp, ys = _ffn_both(yp, ys, ffn1_w)
        yp, hp, rp = _mixer(yp, *_rope_tables(0, sp), hist0, r0, *mix_w,
                            nb=bp, tl=min(MIX_TILE, sp))
        ys, hs, rs = _mixer(ys, *_rope_tables(PAST_LEN, ss), state_conv[l], state_ret[l], *mix_w,
                            nb=bs, tl=ss)
        yp, ys = _ffn_both(yp, ys, ffn2_w)
        conv_p.append(hp); ret_p.append(rp); conv_s.append(hs); ret_s.append(rs)
    return (yp, ys, jnp.stack(conv_p), jnp.stack(ret_p), jnp.stack(conv_s), jnp.stack(ret_s))
```

```python
import functools
import math

import numpy as np

import jax
import jax.numpy as jnp
from jax import lax
from jax.experimental import pallas as pl
from jax.experimental.pallas import tpu as pltpu

D_MODEL = 1024
D_FF = 2816
D_CONV = 512
D_RET = 512
N_RET_HEADS = 4
HEAD_DIM = 128
CONV_WIDTH = 31
HIST = CONV_WIDTH - 1
ROPE_THETA = 10000.0
EPS = 1e-6
D_IN = 2 * D_CONV + 4 * D_RET
PAST_LEN = 4096

V7X_SUBLANES = 8
V7X_LANES = 128
LANE_BLOCKS = D_CONV // V7X_LANES
V7X_VMEM_BYTES = 64 * 1024 * 1024

HIST_PAD = 32
CONV_ROW_BLOCK = 64
FFN_TILE = 1024
MIX_TILE = 512
RET_CHUNK = 256
EPILOGUE_ROWS = 256
V7X_MXU_DIM = 256
FF_CHUNKS = D_FF // V7X_MXU_DIM
assert FF_CHUNKS * V7X_MXU_DIM == D_FF

BF16 = jnp.bfloat16
F32 = jnp.float32


def _vmem_limit(resident_bytes, tile_bytes, temp_bytes):
    need = resident_bytes + 2 * tile_bytes + temp_bytes
    return int(min(need * 5 // 4 + (4 << 20), V7X_VMEM_BYTES - (6 << 20)))


def _rmsnorm(x, g):
    ms = jnp.mean(x * x, axis=-1, keepdims=True)
    return x * lax.rsqrt(ms + EPS) * g


def _lanes(block):
    return slice(block * V7X_LANES, (block + 1) * V7X_LANES)


def _silu(x):
    return x * jax.nn.sigmoid(x)


def _resident(shape):
    nd = len(shape)
    return pl.BlockSpec(shape, lambda *_: (0,) * nd, pipeline_mode=pl.Buffered(1))


def _ffn_chunk(hb, wg_ref, wu_ref, wd_ref, c):
    g = jnp.dot(hb, wg_ref[c], preferred_element_type=F32)
    u = jnp.dot(hb, wu_ref[c], preferred_element_type=F32)
    a = (_silu(g) * u).astype(BF16)
    return jnp.dot(a, wd_ref[c], preferred_element_type=F32)


def _row_blocks(rows):
    rb = min(EPILOGUE_ROWS, rows)
    assert rows % rb == 0
    return [slice(r, r + rb) for r in range(0, rows, rb)]


def _ffn_math(x, gpre_ref, wg_ref, wu_ref, wd_ref, gpost_ref):
    hb = _rmsnorm(x, gpre_ref[...]).astype(BF16)
    acc = _ffn_chunk(hb, wg_ref, wu_ref, wd_ref, 0)
    for c in range(1, FF_CHUNKS):
        acc = acc + _ffn_chunk(hb, wg_ref, wu_ref, wd_ref, c)
    return x + 0.5 * _rmsnorm(acc, gpost_ref[...])


def _ffn_kernel(xp_ref, xs_ref, gpre_ref, wg32_ref, wu32_ref, wd32_ref, gpost_ref,
                yp_ref, ys_ref, wg_ref, wu_ref, wd_ref, hb_ref, acc_ref, *, n_tiles):
    step = pl.program_id(0)
    last = FF_CHUNKS - 1

    @pl.when(step <= last)
    def _():
        @pl.when(step == 0)
        def _():
            hb_ref[...] = _rmsnorm(xp_ref[...], gpre_ref[...]).astype(BF16)

        wg_ref[step] = wg32_ref[...].astype(BF16)
        wu_ref[step] = wu32_ref[...].astype(BF16)
        wd_ref[step] = wd32_ref[...].astype(BF16)
        d = _ffn_chunk(hb_ref[...], wg_ref, wu_ref, wd_ref, step)

        @pl.when(step == 0)
        def _():
            acc_ref[...] = d

        @pl.when(step > 0)
        def _():
            acc_ref[...] += d

        @pl.when(step == last)
        def _():
            yp_ref[...] = xp_ref[...] + 0.5 * _rmsnorm(acc_ref[...], gpost_ref[...])

    @pl.when(jnp.logical_and(step > last, step < last + n_tiles))
    def _():
        yp_ref[...] = _ffn_math(xp_ref[...], gpre_ref, wg_ref, wu_ref, wd_ref, gpost_ref)

    @pl.when(step == last + n_tiles)
    def _():
        ys_ref[...] = _ffn_math(xs_ref[...], gpre_ref, wg_ref, wu_ref, wd_ref, gpost_ref)


def _ffn(xp2d, xs2d, gpre, wg32, wu32, wd32, gpost):
    rows, sample_rows = xp2d.shape[0], xs2d.shape[0]
    tm = FFN_TILE
    assert rows % tm == 0 and sample_rows % V7X_SUBLANES == 0
    n_tiles = rows // tm
    last = FF_CHUNKS - 1
    tile_index = lambda s: (jnp.clip(s - last, 0, n_tiles - 1), 0)
    chunk_index = lambda s: jnp.minimum(s, last)
    prompt_tile = pl.BlockSpec((tm, D_MODEL), tile_index)
    sample_tile = pl.BlockSpec((sample_rows, D_MODEL), lambda s: (0, 0))
    w_bytes = 3 * D_MODEL * D_FF
    return pl.pallas_call(
        functools.partial(_ffn_kernel, n_tiles=n_tiles),
        grid=(last + n_tiles + 1,),
        in_specs=[prompt_tile, sample_tile, _resident((1, D_MODEL)),
                  pl.BlockSpec((D_MODEL, V7X_MXU_DIM), lambda s: (0, chunk_index(s))),
                  pl.BlockSpec((D_MODEL, V7X_MXU_DIM), lambda s: (0, chunk_index(s))),
                  pl.BlockSpec((V7X_MXU_DIM, D_MODEL), lambda s: (chunk_index(s), 0)),
                  _resident((1, D_MODEL))],
        out_specs=[prompt_tile, sample_tile],
        out_shape=[jax.ShapeDtypeStruct((rows, D_MODEL), F32),
                   jax.ShapeDtypeStruct((sample_rows, D_MODEL), F32)],
        scratch_shapes=[pltpu.VMEM((FF_CHUNKS, D_MODEL, V7X_MXU_DIM), BF16),
                        pltpu.VMEM((FF_CHUNKS, D_MODEL, V7X_MXU_DIM), BF16),
                        pltpu.VMEM((FF_CHUNKS, V7X_MXU_DIM, D_MODEL), BF16),
                        pltpu.VMEM((tm, D_MODEL), BF16),
                        pltpu.VMEM((tm, D_MODEL), F32)],
        compiler_params=pltpu.CompilerParams(
            dimension_semantics=("arbitrary",),
            vmem_limit_bytes=_vmem_limit(
                2 * w_bytes + 2 * 3 * D_MODEL * V7X_MXU_DIM * 4 + tm * D_MODEL * 6,
                2 * (tm + sample_rows) * D_MODEL * 4,
                tm * (3 * V7X_MXU_DIM + 3 * D_MODEL) * 4)),
        name="ffn",
    )(xp2d, xs2d, gpre, wg32, wu32, wd32, gpost)


def _mixer_kernel(x_ref, cos_ref, sin_ref, hist0_ref, r0_ref, gpre_ref, win_ref, cw_ref,
                  cb_ref, lng_ref, lnb_ref, gng_ref, wout_ref, gpost_ref,
                  y_ref, hist_ref, rout_ref,
                  uext_ref, rstate_ref, mix_ref, dmask_ref, decay_ref, *, nb, tl, chunk):
    t = pl.program_id(1)
    log_gamma = [math.log1p(-(2.0 ** (-5.0 - h))) for h in range(N_RET_HEADS)]

    @pl.when(t == 0)
    def _init():
        for lb in range(LANE_BLOCKS):
            uext_ref[:, lb, HIST_PAD - HIST:HIST_PAD, :] = hist0_ref[:, :, _lanes(lb)]
        rstate_ref[...] = r0_ref[...]
        ii = lax.broadcasted_iota(jnp.int32, (chunk, chunk), 0)
        jj = lax.broadcasted_iota(jnp.int32, (chunk, chunk), 1)
        rel = (ii - jj).astype(F32)
        row = lax.broadcasted_iota(jnp.int32, (chunk, HEAD_DIM), 0).astype(F32)
        for h in range(N_RET_HEADS):
            dmask_ref[h] = jnp.where(rel >= 0.0, jnp.exp(log_gamma[h] * jnp.maximum(rel, 0.0)), 0.0)
            decay_ref[h, 0] = jnp.exp(log_gamma[h] * (row + 1.0))
            decay_ref[h, 1] = jnp.exp(log_gamma[h] * (chunk - 1.0 - row))

    m = nb * tl
    x = x_ref[...].reshape(m, D_MODEL)
    hb = _rmsnorm(x, gpre_ref[...]).astype(BF16)
    ab = jnp.dot(hb, win_ref[:, :2 * D_CONV], preferred_element_type=F32)
    p = jnp.dot(hb, win_ref[:, 2 * D_CONV:], preferred_element_type=F32)

    u = ab[:, :D_CONV] * jax.nn.sigmoid(ab[:, D_CONV:])
    for s in range(nb):
        for lb in range(LANE_BLOCKS):
            uext_ref[s, lb, HIST_PAD:HIST_PAD + tl, :] = u[s * tl:(s + 1) * tl, _lanes(lb)]
    rb = min(CONV_ROW_BLOCK, tl)
    base = HIST_PAD - HIST
    for s in range(nb):
        for r0 in range(0, tl, rb):
            taps = []
            for lb in range(LANE_BLOCKS):
                acc = jnp.broadcast_to(cb_ref[:, _lanes(lb)], (rb, V7X_LANES))
                for j in range(CONV_WIDTH):
                    acc = acc + (cw_ref[j:j + 1, _lanes(lb)]
                                 * uext_ref[s, lb, r0 + base + j:r0 + base + j + rb, :])
                taps.append(acc)
            acc = jnp.concatenate(taps, axis=-1)
            mu = jnp.mean(acc, axis=-1, keepdims=True)
            dc = acc - mu
            var = jnp.mean(dc * dc, axis=-1, keepdims=True)
            c = _silu(dc * lax.rsqrt(var + EPS) * lng_ref[...] + lnb_ref[...])
            mix_ref[s * tl + r0:s * tl + r0 + rb, 0:D_CONV] = c.astype(BF16)
        for lb in range(LANE_BLOCKS):
            new_hist = uext_ref[s, lb, tl + base:tl + HIST_PAD, :]
            hist_ref[s, :, _lanes(lb)] = new_hist
            uext_ref[s, lb, base:HIST_PAD, :] = new_hist

    cos = cos_ref[...]
    sin = sin_ref[...]
    q0, k0, v0, g0 = (i * D_RET for i in range(4))
    for s in range(nb):
        rows = slice(s * tl, (s + 1) * tl)
        for h in range(N_RET_HEADS):
            hs = slice(h * HEAD_DIM, (h + 1) * HEAD_DIM)
            lane = lambda off: slice(off + h * HEAD_DIM, off + (h + 1) * HEAD_DIM)
            q = p[rows, lane(q0)]
            k = p[rows, lane(k0)]
            v = p[rows, lane(v0)].astype(BF16)
            gate = p[rows, lane(g0)]
            qr = (q * cos + pltpu.roll(q, HEAD_DIM // 2, 1) * sin).astype(BF16)
            kr = (k * cos + pltpu.roll(k, HEAD_DIM // 2, 1) * sin) * (HEAD_DIM ** -0.5)
            state = rstate_ref[s, h]
            g_chunk = math.exp(log_gamma[h] * chunk)
            for ci in range(tl // chunk):
                cs = slice(ci * chunk, (ci + 1) * chunk)
                qc, kc, vc = qr[cs], kr[cs], v[cs]
                sc = lax.dot_general(qc, kc.astype(BF16), (((1,), (1,)), ((), ())),
                                     preferred_element_type=F32) * dmask_ref[h]
                o = (jnp.dot(sc.astype(BF16), vc, preferred_element_type=F32)
                     + jnp.dot(qc, state.astype(BF16), preferred_element_type=F32) * decay_ref[h, 0])
                kz = (kc * decay_ref[h, 1]).astype(BF16)
                state = state * g_chunk + lax.dot_general(
                    kz, vc, (((0,), (0,)), ((), ())), preferred_element_type=F32)
                mu = jnp.mean(o, axis=-1, keepdims=True)
                do = o - mu
                var = jnp.mean(do * do, axis=-1, keepdims=True)
                on = do * lax.rsqrt(var + EPS)
                r = _silu(gate[cs]) * (on * gng_ref[:, hs])
                mix_ref[s * tl + ci * chunk:s * tl + (ci + 1) * chunk,
                        D_CONV + h * HEAD_DIM:D_CONV + (h + 1) * HEAD_DIM] = r.astype(BF16)
            rstate_ref[s, h] = state
            rout_ref[s, h] = state

    y = []
    for rows in _row_blocks(m):
        out = jnp.dot(mix_ref[rows, :], wout_ref[...], preferred_element_type=F32)
        y.append(x[rows] + _rmsnorm(out, gpost_ref[...]))
    y_ref[...] = jnp.concatenate(y, axis=0).reshape(nb, tl, D_MODEL)


def _mixer(x, cos, sin, hist0, r0, gpre, win, cw, cb, lng, lnb, gng, wout, gpost, *, nb, tl):
    bsz, seq, _ = x.shape
    assert bsz % nb == 0 and seq % tl == 0 and tl % V7X_SUBLANES == 0
    chunk = min(RET_CHUNK, tl)
    assert tl % chunk == 0
    m = nb * tl
    body = functools.partial(_mixer_kernel, nb=nb, tl=tl, chunk=chunk)
    weights = (D_MODEL * D_IN + D_MODEL * D_MODEL) * 2
    scratch = (nb * (tl + HIST_PAD) * D_CONV * 4 + nb * N_RET_HEADS * HEAD_DIM * HEAD_DIM * 4
               + m * D_MODEL * 2 + N_RET_HEADS * chunk * (chunk + 2 * HEAD_DIM) * 4)
    return pl.pallas_call(
        body,
        grid=(bsz // nb, seq // tl),
        in_specs=[
            pl.BlockSpec((nb, tl, D_MODEL), lambda b, t: (b, t, 0)),
            pl.BlockSpec((tl, HEAD_DIM), lambda b, t: (t, 0)),
            pl.BlockSpec((tl, HEAD_DIM), lambda b, t: (t, 0)),
            pl.BlockSpec((nb, HIST, D_CONV), lambda b, t: (b, 0, 0)),
            pl.BlockSpec((nb, N_RET_HEADS, HEAD_DIM, HEAD_DIM), lambda b, t: (b, 0, 0, 0)),
            _resident((1, D_MODEL)), _resident((D_MODEL, D_IN)), _resident((CONV_WIDTH, D_CONV)),
            _resident((1, D_CONV)), _resident((1, D_CONV)), _resident((1, D_CONV)),
            _resident((1, D_RET)), _resident((D_MODEL, D_MODEL)), _resident((1, D_MODEL)),
        ],
        out_specs=[
            pl.BlockSpec((nb, tl, D_MODEL), lambda b, t: (b, t, 0)),
            pl.BlockSpec((nb, HIST, D_CONV), lambda b, t: (b, 0, 0)),
            pl.BlockSpec((nb, N_RET_HEADS, HEAD_DIM, HEAD_DIM), lambda b, t: (b, 0, 0, 0)),
        ],
        out_shape=[
            jax.ShapeDtypeStruct((bsz, seq, D_MODEL), F32),
            jax.ShapeDtypeStruct((bsz, HIST, D_CONV), F32),
            jax.ShapeDtypeStruct((bsz, N_RET_HEADS, HEAD_DIM, HEAD_DIM), F32),
        ],
        scratch_shapes=[
            pltpu.VMEM((nb, LANE_BLOCKS, tl + HIST_PAD, V7X_LANES), F32),
            pltpu.VMEM((nb, N_RET_HEADS, HEAD_DIM, HEAD_DIM), F32),
            pltpu.VMEM((m, D_MODEL), BF16),
            pltpu.VMEM((N_RET_HEADS, chunk, chunk), F32),
            pltpu.VMEM((N_RET_HEADS, 2, chunk, HEAD_DIM), F32),
        ],
        compiler_params=pltpu.CompilerParams(
            dimension_semantics=("arbitrary", "arbitrary"),
            vmem_limit_bytes=_vmem_limit(weights + scratch, 2 * m * D_MODEL * 4,
                                         m * (D_IN + 3 * D_MODEL) * 4)),
        name="mixer",
    )(x, cos, sin, hist0, r0, gpre, win, cw, cb, lng, lnb, gng, wout, gpost)


def _rope_tables(pos0, n):
    half = HEAD_DIM // 2
    inv = ROPE_THETA ** (-np.arange(half, dtype=np.float64) / half)
    ang = (pos0 + np.arange(n, dtype=np.float64))[:, None] * inv[None, :]
    cos, sin = np.cos(ang), np.sin(ang)
    return (np.concatenate([cos, cos], axis=-1).astype(np.float32),
            np.concatenate([-sin, sin], axis=-1).astype(np.float32))


def _ffn_both(xp, xs, ffn_w):
    yp, ys = _ffn(xp.reshape(-1, D_MODEL), xs.reshape(-1, D_MODEL), *ffn_w)
    return yp.reshape(xp.shape), ys.reshape(xs.shape)


def kernel(x_prompt, x_sample, state_conv, state_ret, ffn1_norm_pre, ffn1_w_gate, ffn1_w_up, ffn1_w_down, ffn1_norm_post, mix_norm_pre, w_in, conv_w, conv_b, conv_ln_g, conv_ln_b, ret_gn_g, w_out, mix_norm_post, ffn2_norm_pre, ffn2_w_gate, ffn2_w_up, ffn2_w_down, ffn2_norm_post):
    depth = ffn1_norm_pre.shape[0]
    bp, sp, _ = x_prompt.shape
    bs, ss, _ = x_sample.shape
    row = lambda a: a.reshape(1, -1)
    yp, ys = x_prompt, x_sample
    conv_p, ret_p, conv_s, ret_s = [], [], [], []
    for l in range(depth):
        ffn1_w = (row(ffn1_norm_pre[l]), ffn1_w_gate[l], ffn1_w_up[l], ffn1_w_down[l],
                  row(ffn1_norm_post[l]))
        ffn2_w = (row(ffn2_norm_pre[l]), ffn2_w_gate[l], ffn2_w_up[l], ffn2_w_down[l],
                  row(ffn2_norm_post[l]))
        mix_w = (row(mix_norm_pre[l]), w_in[l].astype(BF16), conv_w[l], row(conv_b[l]),
                 row(conv_ln_g[l]), row(conv_ln_b[l]), row(ret_gn_g[l]), w_out[l].astype(BF16),
                 row(mix_norm_post[l]))
        hist0 = jnp.zeros((bp, HIST, D_CONV), F32)
        r0 = jnp.zeros((bp, N_RET_HEADS, HEAD_DIM, HEAD_DIM), F32)
        yp, ys = _ffn_both(yp, ys, ffn1_w)
        yp, hp, rp = _mixer(yp, *_rope_tables(0, sp), hist0, r0, *mix_w,
                            nb=bp, tl=min(MIX_TILE, sp))
        ys, hs, rs = _mixer(ys, *_rope_tables(PAST_LEN, ss), state_conv[l], state_ret[l], *mix_w,
                            nb=bs, tl=ss)
        yp, ys = _ffn_both(yp, ys, ffn2_w)
        conv_p.append(hp); ret_p.append(rp); conv_s.append(hs); ret_s.append(rs)
    return (yp, ys, jnp.stack(conv_p), jnp.stack(ret_p), jnp.stack(conv_s), jnp.stack(ret_s))
```

```python
import functools
import math

import numpy as np

import jax
import jax.numpy as jnp
from jax import lax
from jax.experimental import pallas as pl
from jax.experimental.pallas import tpu as pltpu

D_MODEL = 1024
D_FF = 2816
D_CONV = 512
D_RET = 512
N_RET_HEADS = 4
HEAD_DIM = 128
CONV_WIDTH = 31
HIST = CONV_WIDTH - 1
ROPE_THETA = 10000.0
EPS = 1e-6
D_IN = 2 * D_CONV + 4 * D_RET
PAST_LEN = 4096

V7X_SUBLANES = 8
V7X_LANES = 128
LANE_BLOCKS = D_CONV // V7X_LANES
V7X_VMEM_BYTES = 64 * 1024 * 1024

HIST_PAD = 32
CONV_ROW_BLOCK = 64
FFN_TILE = 1024
MIX_TILE = 512
RET_CHUNK = 256
MIX_WEIGHT_STEPS = 4
EPILOGUE_ROWS = 256
V7X_MXU_DIM = 256
FF_CHUNKS = D_FF // V7X_MXU_DIM
assert FF_CHUNKS * V7X_MXU_DIM == D_FF

BF16 = jnp.bfloat16
F32 = jnp.float32


def _vmem_limit(resident_bytes, tile_bytes, temp_bytes):
    need = resident_bytes + 2 * tile_bytes + temp_bytes
    return int(min(need * 5 // 4 + (4 << 20), V7X_VMEM_BYTES - (6 << 20)))


def _rmsnorm(x, g):
    ms = jnp.mean(x * x, axis=-1, keepdims=True)
    return x * lax.rsqrt(ms + EPS) * g


def _lanes(block):
    return slice(block * V7X_LANES, (block + 1) * V7X_LANES)


def _silu(x):
    return x * jax.nn.sigmoid(x)


def _resident(shape):
    nd = len(shape)
    return pl.BlockSpec(shape, lambda *_: (0,) * nd, pipeline_mode=pl.Buffered(1))


def _ffn_chunk(hb, wg_ref, wu_ref, wd_ref, c):
    g = jnp.dot(hb, wg_ref[c], preferred_element_type=F32)
    u = jnp.dot(hb, wu_ref[c], preferred_element_type=F32)
    a = (_silu(g) * u).astype(BF16)
    return jnp.dot(a, wd_ref[c], preferred_element_type=F32)


def _row_blocks(rows):
    rb = min(EPILOGUE_ROWS, rows)
    assert rows % rb == 0
    return [slice(r, r + rb) for r in range(0, rows, rb)]


def _ffn_math(x, gpre_ref, wg_ref, wu_ref, wd_ref, gpost_ref):
    hb = _rmsnorm(x, gpre_ref[...]).astype(BF16)
    acc = _ffn_chunk(hb, wg_ref, wu_ref, wd_ref, 0)
    for c in range(1, FF_CHUNKS):
        acc = acc + _ffn_chunk(hb, wg_ref, wu_ref, wd_ref, c)
    return x + 0.5 * _rmsnorm(acc, gpost_ref[...])


def _ffn_kernel(xp_ref, xs_ref, gpre_ref, wg32_ref, wu32_ref, wd32_ref, gpost_ref,
                yp_ref, ys_ref, wg_ref, wu_ref, wd_ref, hb_ref, acc_ref, *, n_tiles):
    step = pl.program_id(0)
    last = FF_CHUNKS - 1

    @pl.when(step <= last)
    def _():
        @pl.when(step == 0)
        def _():
            hb_ref[...] = _rmsnorm(xp_ref[...], gpre_ref[...]).astype(BF16)

        wg_ref[step] = wg32_ref[...].astype(BF16)
        wu_ref[step] = wu32_ref[...].astype(BF16)
        wd_ref[step] = wd32_ref[...].astype(BF16)
        d = _ffn_chunk(hb_ref[...], wg_ref, wu_ref, wd_ref, step)

        @pl.when(step == 0)
        def _():
            acc_ref[...] = d

        @pl.when(step > 0)
        def _():
            acc_ref[...] += d

        @pl.when(step == last)
        def _():
            yp_ref[...] = xp_ref[...] + 0.5 * _rmsnorm(acc_ref[...], gpost_ref[...])

    @pl.when(jnp.logical_and(step > last, step < last + n_tiles))
    def _():
        yp_ref[...] = _ffn_math(xp_ref[...], gpre_ref, wg_ref, wu_ref, wd_ref, gpost_ref)

    @pl.when(step == last + n_tiles)
    def _():
        ys_ref[...] = _ffn_math(xs_ref[...], gpre_ref, wg_ref, wu_ref, wd_ref, gpost_ref)


def _ffn(xp2d, xs2d, gpre, wg32, wu32, wd32, gpost):
    rows, sample_rows = xp2d.shape[0], xs2d.shape[0]
    tm = FFN_TILE
    assert rows % tm == 0 and sample_rows % V7X_SUBLANES == 0
    n_tiles = rows // tm
    last = FF_CHUNKS - 1
    tile_index = lambda s: (jnp.clip(s - last, 0, n_tiles - 1), 0)
    chunk_index = lambda s: jnp.minimum(s, last)
    prompt_tile = pl.BlockSpec((tm, D_MODEL), tile_index)
    sample_tile = pl.BlockSpec((sample_rows, D_MODEL), lambda s: (0, 0))
    w_bytes = 3 * D_MODEL * D_FF
    return pl.pallas_call(
        functools.partial(_ffn_kernel, n_tiles=n_tiles),
        grid=(last + n_tiles + 1,),
        in_specs=[prompt_tile, sample_tile, _resident((1, D_MODEL)),
                  pl.BlockSpec((D_MODEL, V7X_MXU_DIM), lambda s: (0, chunk_index(s))),
                  pl.BlockSpec((D_MODEL, V7X_MXU_DIM), lambda s: (0, chunk_index(s))),
                  pl.BlockSpec((V7X_MXU_DIM, D_MODEL), lambda s: (chunk_index(s), 0)),
                  _resident((1, D_MODEL))],
        out_specs=[prompt_tile, sample_tile],
        out_shape=[jax.ShapeDtypeStruct((rows, D_MODEL), F32),
                   jax.ShapeDtypeStruct((sample_rows, D_MODEL), F32)],
        scratch_shapes=[pltpu.VMEM((FF_CHUNKS, D_MODEL, V7X_MXU_DIM), BF16),
                        pltpu.VMEM((FF_CHUNKS, D_MODEL, V7X_MXU_DIM), BF16),
                        pltpu.VMEM((FF_CHUNKS, V7X_MXU_DIM, D_MODEL), BF16),
                        pltpu.VMEM((tm, D_MODEL), BF16),
                        pltpu.VMEM((tm, D_MODEL), F32)],
        compiler_params=pltpu.CompilerParams(
            dimension_semantics=("arbitrary",),
            vmem_limit_bytes=_vmem_limit(
                2 * w_bytes + 2 * 3 * D_MODEL * V7X_MXU_DIM * 4 + tm * D_MODEL * 6,
                2 * (tm + sample_rows) * D_MODEL * 4,
                tm * (3 * V7X_MXU_DIM + 3 * D_MODEL) * 4)),
        name="ffn",
    )(xp2d, xs2d, gpre, wg32, wu32, wd32, gpost)


def _cast_rows(step, src_ref, out_ref, resident_ref):
    rows = src_ref.shape[0]
    block = src_ref[...].astype(BF16)
    out_ref[...] = block
    resident_ref[pl.ds(pl.multiple_of(step * rows, rows), rows), :] = block


def _mixer_kernel(*refs, nb, tl, chunk):
    _mixer_tile(pl.program_id(1), *refs, nb=nb, tl=tl, chunk=chunk)


def _mixer_cast_kernel(x_ref, cos_ref, sin_ref, hist0_ref, r0_ref, gpre_ref, win32_ref, cw_ref,
                       cb_ref, lng_ref, lnb_ref, gng_ref, wout32_ref, gpost_ref,
                       y_ref, hist_ref, rout_ref, win16_ref, wout16_ref,
                       uext_ref, rstate_ref, mix_ref, dmask_ref, decay_ref, win_ref, wout_ref, *,
                       nb, tl, chunk):
    step = pl.program_id(1)

    @pl.when(step < MIX_WEIGHT_STEPS)
    def _():
        _cast_rows(step, win32_ref, win16_ref, win_ref)
        _cast_rows(step, wout32_ref, wout16_ref, wout_ref)

    @pl.when(step >= MIX_WEIGHT_STEPS)
    def _():
        _mixer_tile(step - MIX_WEIGHT_STEPS, x_ref, cos_ref, sin_ref, hist0_ref, r0_ref, gpre_ref,
                    win_ref, cw_ref, cb_ref, lng_ref, lnb_ref, gng_ref, wout_ref, gpost_ref,
                    y_ref, hist_ref, rout_ref, uext_ref, rstate_ref, mix_ref, dmask_ref,
                    decay_ref, nb=nb, tl=tl, chunk=chunk)


def _mixer_tile(t, x_ref, cos_ref, sin_ref, hist0_ref, r0_ref, gpre_ref, win_ref, cw_ref,
                cb_ref, lng_ref, lnb_ref, gng_ref, wout_ref, gpost_ref,
                y_ref, hist_ref, rout_ref,
                uext_ref, rstate_ref, mix_ref, dmask_ref, decay_ref, *, nb, tl, chunk):
    log_gamma = [math.log1p(-(2.0 ** (-5.0 - h))) for h in range(N_RET_HEADS)]

    @pl.when(t == 0)
    def _init():
        for lb in range(LANE_BLOCKS):
            uext_ref[:, lb, HIST_PAD - HIST:HIST_PAD, :] = hist0_ref[:, :, _lanes(lb)]
        rstate_ref[...] = r0_ref[...]
        ii = lax.broadcasted_iota(jnp.int32, (chunk, chunk), 0)
        jj = lax.broadcasted_iota(jnp.int32, (chunk, chunk), 1)
        rel = (ii - jj).astype(F32)
        row = lax.broadcasted_iota(jnp.int32, (chunk, HEAD_DIM), 0).astype(F32)
        for h in range(N_RET_HEADS):
            dmask_ref[h] = jnp.where(rel >= 0.0, jnp.exp(log_gamma[h] * jnp.maximum(rel, 0.0)), 0.0)
            decay_ref[h, 0] = jnp.exp(log_gamma[h] * (row + 1.0))
            decay_ref[h, 1] = jnp.exp(log_gamma[h] * (chunk - 1.0 - row))

    m = nb * tl
    x = x_ref[...].reshape(m, D_MODEL)
    hb = _rmsnorm(x, gpre_ref[...]).astype(BF16)
    ab = jnp.dot(hb, win_ref[:, :2 * D_CONV], preferred_element_type=F32)
    p = jnp.dot(hb, win_ref[:, 2 * D_CONV:], preferred_element_type=F32)

    u = ab[:, :D_CONV] * jax.nn.sigmoid(ab[:, D_CONV:])
    for s in range(nb):
        for lb in range(LANE_BLOCKS):
            uext_ref[s, lb, HIST_PAD:HIST_PAD + tl, :] = u[s * tl:(s + 1) * tl, _lanes(lb)]
    rb = min(CONV_ROW_BLOCK, tl)
    base = HIST_PAD - HIST
    for s in range(nb):
        for r0 in range(0, tl, rb):
            taps = []
            for lb in range(LANE_BLOCKS):
                acc = jnp.broadcast_to(cb_ref[:, _lanes(lb)], (rb, V7X_LANES))
                for j in range(CONV_WIDTH):
                    acc = acc + (cw_ref[j:j + 1, _lanes(lb)]
                                 * uext_ref[s, lb, r0 + base + j:r0 + base + j + rb, :])
                taps.append(acc)
            acc = jnp.concatenate(taps, axis=-1)
            mu = jnp.mean(acc, axis=-1, keepdims=True)
            dc = acc - mu
            var = jnp.mean(dc * dc, axis=-1, keepdims=True)
            c = _silu(dc * lax.rsqrt(var + EPS) * lng_ref[...] + lnb_ref[...])
            mix_ref[s * tl + r0:s * tl + r0 + rb, 0:D_CONV] = c.astype(BF16)
        for lb in range(LANE_BLOCKS):
            new_hist = uext_ref[s, lb, tl + base:tl + HIST_PAD, :]
            hist_ref[s, :, _lanes(lb)] = new_hist
            uext_ref[s, lb, base:HIST_PAD, :] = new_hist

    cos = cos_ref[...]
    sin = sin_ref[...]
    q0, k0, v0, g0 = (i * D_RET for i in range(4))
    for s in range(nb):
        rows = slice(s * tl, (s + 1) * tl)
        for h in range(N_RET_HEADS):
            hs = slice(h * HEAD_DIM, (h + 1) * HEAD_DIM)
            lane = lambda off: slice(off + h * HEAD_DIM, off + (h + 1) * HEAD_DIM)
            q = p[rows, lane(q0)]
            k = p[rows, lane(k0)]
            v = p[rows, lane(v0)].astype(BF16)
            gate = p[rows, lane(g0)]
            qr = (q * cos + pltpu.roll(q, HEAD_DIM // 2, 1) * sin).astype(BF16)
            kr = (k * cos + pltpu.roll(k, HEAD_DIM // 2, 1) * sin) * (HEAD_DIM ** -0.5)
            state = rstate_ref[s, h]
            g_chunk = math.exp(log_gamma[h] * chunk)
            for ci in range(tl // chunk):
                cs = slice(ci * chunk, (ci + 1) * chunk)
                qc, kc, vc = qr[cs], kr[cs], v[cs]
                sc = lax.dot_general(qc, kc.astype(BF16), (((1,), (1,)), ((), ())),
                                     preferred_element_type=F32) * dmask_ref[h]
                o = (jnp.dot(sc.astype(BF16), vc, preferred_element_type=F32)
                     + jnp.dot(qc, state.astype(BF16), preferred_element_type=F32) * decay_ref[h, 0])
                kz = (kc * decay_ref[h, 1]).astype(BF16)
                state = state * g_chunk + lax.dot_general(
                    kz, vc, (((0,), (0,)), ((), ())), preferred_element_type=F32)
                mu = jnp.mean(o, axis=-1, keepdims=True)
                do = o - mu
                var = jnp.mean(do * do, axis=-1, keepdims=True)
                on = do * lax.rsqrt(var + EPS)
                r = _silu(gate[cs]) * (on * gng_ref[:, hs])
                mix_ref[s * tl + ci * chunk:s * tl + (ci + 1) * chunk,
                        D_CONV + h * HEAD_DIM:D_CONV + (h + 1) * HEAD_DIM] = r.astype(BF16)
            rstate_ref[s, h] = state
            rout_ref[s, h] = state

    y = []
    for rows in _row_blocks(m):
        out = jnp.dot(mix_ref[rows, :], wout_ref[...], preferred_element_type=F32)
        y.append(x[rows] + _rmsnorm(out, gpost_ref[...]))
    y_ref[...] = jnp.concatenate(y, axis=0).reshape(nb, tl, D_MODEL)


def _mixer(x, cos, sin, hist0, r0, gpre, win, cw, cb, lng, lnb, gng, wout, gpost, *, nb, tl,
           cast):
    bsz, seq, _ = x.shape
    assert bsz % nb == 0 and seq % tl == 0 and tl % V7X_SUBLANES == 0
    chunk = min(RET_CHUNK, tl)
    assert tl % chunk == 0
    m = nb * tl
    wsteps = MIX_WEIGHT_STEPS if cast else 0
    tile = lambda t: jnp.maximum(t - wsteps, 0)
    weights = (D_MODEL * D_IN + D_MODEL * D_MODEL) * 2
    scratch = (nb * (tl + HIST_PAD) * D_CONV * 4 + nb * N_RET_HEADS * HEAD_DIM * HEAD_DIM * 4
               + m * D_MODEL * 2 + N_RET_HEADS * chunk * (chunk + 2 * HEAD_DIM) * 4)
    out_specs = [
        pl.BlockSpec((nb, tl, D_MODEL), lambda b, t: (b, tile(t), 0)),
        pl.BlockSpec((nb, HIST, D_CONV), lambda b, t: (b, 0, 0)),
        pl.BlockSpec((nb, N_RET_HEADS, HEAD_DIM, HEAD_DIM), lambda b, t: (b, 0, 0, 0)),
    ]
    out_shape = [
        jax.ShapeDtypeStruct((bsz, seq, D_MODEL), F32),
        jax.ShapeDtypeStruct((bsz, HIST, D_CONV), F32),
        jax.ShapeDtypeStruct((bsz, N_RET_HEADS, HEAD_DIM, HEAD_DIM), F32),
    ]
    scratch_shapes = [
        pltpu.VMEM((nb, LANE_BLOCKS, tl + HIST_PAD, V7X_LANES), F32),
        pltpu.VMEM((nb, N_RET_HEADS, HEAD_DIM, HEAD_DIM), F32),
        pltpu.VMEM((m, D_MODEL), BF16),
        pltpu.VMEM((N_RET_HEADS, chunk, chunk), F32),
        pltpu.VMEM((N_RET_HEADS, 2, chunk, HEAD_DIM), F32),
    ]
    if cast:
        assert bsz == nb, "the weight steps run once, on the only sequence group"
        rows = D_MODEL // MIX_WEIGHT_STEPS
        row_block = lambda b, t: (jnp.minimum(t, MIX_WEIGHT_STEPS - 1), 0)
        win_spec = pl.BlockSpec((rows, D_IN), row_block)
        wout_spec = pl.BlockSpec((rows, D_MODEL), row_block)
        out_specs += [win_spec, wout_spec]
        out_shape += [jax.ShapeDtypeStruct((D_MODEL, D_IN), BF16),
                      jax.ShapeDtypeStruct((D_MODEL, D_MODEL), BF16)]
        scratch_shapes += [pltpu.VMEM((D_MODEL, D_IN), BF16), pltpu.VMEM((D_MODEL, D_MODEL), BF16)]
        weights += 2 * rows * (D_IN + D_MODEL) * 6
        body = _mixer_cast_kernel
    else:
        win_spec, wout_spec = _resident((D_MODEL, D_IN)), _resident((D_MODEL, D_MODEL))
        body = _mixer_kernel
    return pl.pallas_call(
        functools.partial(body, nb=nb, tl=tl, chunk=chunk),
        grid=(bsz // nb, wsteps + seq // tl),
        in_specs=[
            pl.BlockSpec((nb, tl, D_MODEL), lambda b, t: (b, tile(t), 0)),
            pl.BlockSpec((tl, HEAD_DIM), lambda b, t: (tile(t), 0)),
            pl.BlockSpec((tl, HEAD_DIM), lambda b, t: (tile(t), 0)),
            pl.BlockSpec((nb, HIST, D_CONV), lambda b, t: (b, 0, 0)),
            pl.BlockSpec((nb, N_RET_HEADS, HEAD_DIM, HEAD_DIM), lambda b, t: (b, 0, 0, 0)),
            _resident((1, D_MODEL)), win_spec, _resident((CONV_WIDTH, D_CONV)),
            _resident((1, D_CONV)), _resident((1, D_CONV)), _resident((1, D_CONV)),
            _resident((1, D_RET)), wout_spec, _resident((1, D_MODEL)),
        ],
        out_specs=out_specs,
        out_shape=out_shape,
        scratch_shapes=scratch_shapes,
        compiler_params=pltpu.CompilerParams(
            dimension_semantics=("arbitrary", "arbitrary"),
            vmem_limit_bytes=_vmem_limit(weights + scratch, 2 * m * D_MODEL * 4,
                                         m * (D_IN + 3 * D_MODEL) * 4)),
        name="mixer",
    )(x, cos, sin, hist0, r0, gpre, win, cw, cb, lng, lnb, gng, wout, gpost)


def _rope_tables(pos0, n):
    half = HEAD_DIM // 2
    inv = ROPE_THETA ** (-np.arange(half, dtype=np.float64) / half)
    ang = (pos0 + np.arange(n, dtype=np.float64))[:, None] * inv[None, :]
    cos, sin = np.cos(ang), np.sin(ang)
    return (np.concatenate([cos, cos], axis=-1).astype(np.float32),
            np.concatenate([-sin, sin], axis=-1).astype(np.float32))


def _ffn_both(xp, xs, ffn_w):
    yp, ys = _ffn(xp.reshape(-1, D_MODEL), xs.reshape(-1, D_MODEL), *ffn_w)
    return yp.reshape(xp.shape), ys.reshape(xs.shape)


def kernel(x_prompt, x_sample, state_conv, state_ret, ffn1_norm_pre, ffn1_w_gate, ffn1_w_up, ffn1_w_down, ffn1_norm_post, mix_norm_pre, w_in, conv_w, conv_b, conv_ln_g, conv_ln_b, ret_gn_g, w_out, mix_norm_post, ffn2_norm_pre, ffn2_w_gate, ffn2_w_up, ffn2_w_down, ffn2_norm_post):
    depth = ffn1_norm_pre.shape[0]
    bp, sp, _ = x_prompt.shape
    bs, ss, _ = x_sample.shape
    row = lambda a: a.reshape(1, -1)
    yp, ys = x_prompt, x_sample
    conv_p, ret_p, conv_s, ret_s = [], [], [], []
    for l in range(depth):
        ffn1_w = (row(ffn1_norm_pre[l]), ffn1_w_gate[l], ffn1_w_up[l], ffn1_w_down[l],
                  row(ffn1_norm_post[l]))
        ffn2_w = (row(ffn2_norm_pre[l]), ffn2_w_gate[l], ffn2_w_up[l], ffn2_w_down[l],
                  row(ffn2_norm_post[l]))
        mix_w = [row(mix_norm_pre[l]), w_in[l], conv_w[l], row(conv_b[l]), row(conv_ln_g[l]),
                 row(conv_ln_b[l]), row(ret_gn_g[l]), w_out[l], row(mix_norm_post[l])]
        hist0 = jnp.zeros((bp, HIST, D_CONV), F32)
        r0 = jnp.zeros((bp, N_RET_HEADS, HEAD_DIM, HEAD_DIM), F32)
        yp, ys = _ffn_both(yp, ys, ffn1_w)
        yp, hp, rp, win_bf16, wout_bf16 = _mixer(yp, *_rope_tables(0, sp), hist0, r0, *mix_w,
                                                 nb=bp, tl=min(MIX_TILE, sp), cast=True)
        mix_w[1], mix_w[7] = win_bf16, wout_bf16
        ys, hs, rs = _mixer(ys, *_rope_tables(PAST_LEN, ss), state_conv[l], state_ret[l], *mix_w,
                            nb=bs, tl=ss, cast=False)
        yp, ys = _ffn_both(yp, ys, ffn2_w)
        conv_p.append(hp); ret_p.append(rp); conv_s.append(hs); ret_s.append(rs)
    return (yp, ys, jnp.stack(conv_p), jnp.stack(ret_p), jnp.stack(conv_s), jnp.stack(ret_s))
```

```python
import functools
import math

import numpy as np

import jax
import jax.numpy as jnp
from jax import lax
from jax.experimental import pallas as pl
from jax.experimental.pallas import tpu as pltpu

D_MODEL = 1024
D_FF = 2816
D_CONV = 512
D_RET = 512
N_RET_HEADS = 4
HEAD_DIM = 128
CONV_WIDTH = 31
HIST = CONV_WIDTH - 1
ROPE_THETA = 10000.0
EPS = 1e-6
D_IN = 2 * D_CONV + 4 * D_RET
PAST_LEN = 4096

V7X_SUBLANES = 8
V7X_LANES = 128
LANE_BLOCKS = D_CONV // V7X_LANES
V7X_VMEM_BYTES = 64 * 1024 * 1024

HIST_PAD = 32
CONV_ROW_BLOCK = 64
FFN_TILE = 1024
MIX_TILE = 512
RET_CHUNK = 256
MIX_WEIGHT_STEPS = 4
EPILOGUE_ROWS = 256
V7X_MXU_DIM = 256
FF_CHUNKS = D_FF // V7X_MXU_DIM
assert FF_CHUNKS * V7X_MXU_DIM == D_FF

BF16 = jnp.bfloat16
F32 = jnp.float32


def _vmem_limit(resident_bytes, tile_bytes, temp_bytes):
    need = resident_bytes + 2 * tile_bytes + temp_bytes
    return int(min(need * 5 // 4 + (4 << 20), V7X_VMEM_BYTES - (6 << 20)))


def _rmsnorm(x, g):
    ms = jnp.mean(x * x, axis=-1, keepdims=True)
    return x * lax.rsqrt(ms + EPS) * g


def _lanes(block):
    return slice(block * V7X_LANES, (block + 1) * V7X_LANES)


def _silu(x):
    return x * jax.nn.sigmoid(x)


def _resident(shape):
    nd = len(shape)
    return pl.BlockSpec(shape, lambda *_: (0,) * nd, pipeline_mode=pl.Buffered(1))


def _ffn_chunk(hb, wg_ref, wu_ref, wd_ref, c):
    g = jnp.dot(hb, wg_ref[c], preferred_element_type=F32)
    u = jnp.dot(hb, wu_ref[c], preferred_element_type=F32)
    a = (_silu(g) * u).astype(BF16)
    return jnp.dot(a, wd_ref[c], preferred_element_type=F32)


def _row_blocks(rows):
    rb = min(EPILOGUE_ROWS, rows)
    assert rows % rb == 0
    return [slice(r, r + rb) for r in range(0, rows, rb)]


def _ffn_math(x, gpre_ref, wg_ref, wu_ref, wd_ref, gpost_ref):
    hb = _rmsnorm(x, gpre_ref[...]).astype(BF16)
    acts = []
    for c in range(FF_CHUNKS):
        g = jnp.dot(hb, wg_ref[c], preferred_element_type=F32)
        u = jnp.dot(hb, wu_ref[c], preferred_element_type=F32)
        acts.append((_silu(g) * u).astype(BF16))
    a = jnp.concatenate(acts, axis=1)
    d = jnp.dot(a, wd_ref[...].reshape(D_FF, D_MODEL), preferred_element_type=F32)
    return x + 0.5 * _rmsnorm(d, gpost_ref[...])


def _ffn_kernel(xp_ref, xs_ref, gpre_ref, wg32_ref, wu32_ref, wd32_ref, gpost_ref,
                yp_ref, ys_ref, wg_ref, wu_ref, wd_ref, hb_ref, acc_ref, *, n_tiles):
    step = pl.program_id(0)
    last = FF_CHUNKS - 1

    @pl.when(step <= last)
    def _():
        @pl.when(step == 0)
        def _():
            hb_ref[...] = _rmsnorm(xp_ref[...], gpre_ref[...]).astype(BF16)
            acc_ref[...] = jnp.zeros_like(acc_ref)

        wg_ref[step] = wg32_ref[...].astype(BF16)
        wu_ref[step] = wu32_ref[...].astype(BF16)
        wd_ref[step] = wd32_ref[...].astype(BF16)
        acc_ref[...] += _ffn_chunk(hb_ref[...], wg_ref, wu_ref, wd_ref, step)

        @pl.when(step == last)
        def _():
            yp_ref[...] = xp_ref[...] + 0.5 * _rmsnorm(acc_ref[...], gpost_ref[...])

    @pl.when(jnp.logical_and(step > last, step < last + n_tiles))
    def _():
        yp_ref[...] = _ffn_math(xp_ref[...], gpre_ref, wg_ref, wu_ref, wd_ref, gpost_ref)

    @pl.when(step == last + n_tiles)
    def _():
        ys_ref[...] = _ffn_math(xs_ref[...], gpre_ref, wg_ref, wu_ref, wd_ref, gpost_ref)


def _ffn(xp2d, xs2d, gpre, wg32, wu32, wd32, gpost):
    rows, sample_rows = xp2d.shape[0], xs2d.shape[0]
    tm = FFN_TILE
    assert rows % tm == 0 and sample_rows % V7X_SUBLANES == 0
    n_tiles = rows // tm
    last = FF_CHUNKS - 1
    tile_index = lambda s: (jnp.clip(s - last, 0, n_tiles - 1), 0)
    chunk_index = lambda s: jnp.minimum(s, last)
    prompt_tile = pl.BlockSpec((tm, D_MODEL), tile_index)
    sample_tile = pl.BlockSpec((sample_rows, D_MODEL), lambda s: (0, 0))
    w_bytes = 3 * D_MODEL * D_FF
    return pl.pallas_call(
        functools.partial(_ffn_kernel, n_tiles=n_tiles),
        grid=(last + n_tiles + 1,),
        in_specs=[prompt_tile, sample_tile, _resident((1, D_MODEL)),
                  pl.BlockSpec((D_MODEL, V7X_MXU_DIM), lambda s: (0, chunk_index(s))),
                  pl.BlockSpec((D_MODEL, V7X_MXU_DIM), lambda s: (0, chunk_index(s))),
                  pl.BlockSpec((V7X_MXU_DIM, D_MODEL), lambda s: (chunk_index(s), 0)),
                  _resident((1, D_MODEL))],
        out_specs=[prompt_tile, sample_tile],
        out_shape=[jax.ShapeDtypeStruct((rows, D_MODEL), F32),
                   jax.ShapeDtypeStruct((sample_rows, D_MODEL), F32)],
        scratch_shapes=[pltpu.VMEM((FF_CHUNKS, D_MODEL, V7X_MXU_DIM), BF16),
                        pltpu.VMEM((FF_CHUNKS, D_MODEL, V7X_MXU_DIM), BF16),
                        pltpu.VMEM((FF_CHUNKS, V7X_MXU_DIM, D_MODEL), BF16),
                        pltpu.VMEM((tm, D_MODEL), BF16),
                        pltpu.VMEM((tm, D_MODEL), F32)],
        compiler_params=pltpu.CompilerParams(
            dimension_semantics=("arbitrary",),
            vmem_limit_bytes=_vmem_limit(
                2 * w_bytes + 2 * 3 * D_MODEL * V7X_MXU_DIM * 4 + tm * D_MODEL * 6,
                2 * (tm + sample_rows) * D_MODEL * 4,
                tm * (3 * V7X_MXU_DIM + 3 * D_MODEL) * 4)),
        name="ffn",
    )(xp2d, xs2d, gpre, wg32, wu32, wd32, gpost)


def _cast_rows(step, src_ref, out_ref, resident_ref):
    rows = src_ref.shape[0]
    block = src_ref[...].astype(BF16)
    out_ref[...] = block
    resident_ref[pl.ds(pl.multiple_of(step * rows, rows), rows), :] = block


def _mixer_kernel(*refs, nb, tl, chunk):
    _mixer_tile(pl.program_id(1), *refs, nb=nb, tl=tl, chunk=chunk)


def _mixer_cast_kernel(x_ref, cos_ref, sin_ref, hist0_ref, r0_ref, gpre_ref, win32_ref, cw_ref,
                       cb_ref, lng_ref, lnb_ref, gng_ref, wout32_ref, gpost_ref,
                       y_ref, hist_ref, rout_ref, win16_ref, wout16_ref,
                       uext_ref, rstate_ref, mix_ref, dmask_ref, decay_ref, win_ref, wout_ref, *,
                       nb, tl, chunk):
    step = pl.program_id(1)

    @pl.when(step < MIX_WEIGHT_STEPS)
    def _():
        _cast_rows(step, win32_ref, win16_ref, win_ref)
        _cast_rows(step, wout32_ref, wout16_ref, wout_ref)

    @pl.when(step >= MIX_WEIGHT_STEPS)
    def _():
        _mixer_tile(step - MIX_WEIGHT_STEPS, x_ref, cos_ref, sin_ref, hist0_ref, r0_ref, gpre_ref,
                    win_ref, cw_ref, cb_ref, lng_ref, lnb_ref, gng_ref, wout_ref, gpost_ref,
                    y_ref, hist_ref, rout_ref, uext_ref, rstate_ref, mix_ref, dmask_ref,
                    decay_ref, nb=nb, tl=tl, chunk=chunk)


def _mixer_tile(t, x_ref, cos_ref, sin_ref, hist0_ref, r0_ref, gpre_ref, win_ref, cw_ref,
                cb_ref, lng_ref, lnb_ref, gng_ref, wout_ref, gpost_ref,
                y_ref, hist_ref, rout_ref,
                uext_ref, rstate_ref, mix_ref, dmask_ref, decay_ref, *, nb, tl, chunk):
    log_gamma = [math.log1p(-(2.0 ** (-5.0 - h))) for h in range(N_RET_HEADS)]

    @pl.when(t == 0)
    def _init():
        for lb in range(LANE_BLOCKS):
            uext_ref[:, lb, HIST_PAD - HIST:HIST_PAD, :] = hist0_ref[:, :, _lanes(lb)]
        rstate_ref[...] = r0_ref[...]
        ii = lax.broadcasted_iota(jnp.int32, (chunk, chunk), 0)
        jj = lax.broadcasted_iota(jnp.int32, (chunk, chunk), 1)
        rel = (ii - jj).astype(F32)
        row = lax.broadcasted_iota(jnp.int32, (chunk, HEAD_DIM), 0).astype(F32)
        for h in range(N_RET_HEADS):
            dmask_ref[h] = jnp.where(rel >= 0.0, jnp.exp(log_gamma[h] * jnp.maximum(rel, 0.0)), 0.0)
            decay_ref[h, 0] = jnp.exp(log_gamma[h] * (row + 1.0))
            decay_ref[h, 1] = jnp.exp(log_gamma[h] * (chunk - 1.0 - row))

    m = nb * tl
    x = x_ref[...].reshape(m, D_MODEL)
    hb = _rmsnorm(x, gpre_ref[...]).astype(BF16)
    ab = jnp.dot(hb, win_ref[:, :2 * D_CONV], preferred_element_type=F32)
    p = jnp.dot(hb, win_ref[:, 2 * D_CONV:], preferred_element_type=F32)

    u = ab[:, :D_CONV] * jax.nn.sigmoid(ab[:, D_CONV:])
    for s in range(nb):
        for lb in range(LANE_BLOCKS):
            uext_ref[s, lb, HIST_PAD:HIST_PAD + tl, :] = u[s * tl:(s + 1) * tl, _lanes(lb)]
    rb = min(CONV_ROW_BLOCK, tl)
    base = HIST_PAD - HIST
    for s in range(nb):
        for r0 in range(0, tl, rb):
            taps = []
            for lb in range(LANE_BLOCKS):
                acc = jnp.broadcast_to(cb_ref[:, _lanes(lb)], (rb, V7X_LANES))
                for j in range(CONV_WIDTH):
                    acc = acc + (cw_ref[j:j + 1, _lanes(lb)]
                                 * uext_ref[s, lb, r0 + base + j:r0 + base + j + rb, :])
                taps.append(acc)
            acc = jnp.concatenate(taps, axis=-1)
            mu = jnp.mean(acc, axis=-1, keepdims=True)
            dc = acc - mu
            var = jnp.mean(dc * dc, axis=-1, keepdims=True)
            c = _silu(dc * lax.rsqrt(var + EPS) * lng_ref[...] + lnb_ref[...])
            mix_ref[s * tl + r0:s * tl + r0 + rb, 0:D_CONV] = c.astype(BF16)
        for lb in range(LANE_BLOCKS):
            new_hist = uext_ref[s, lb, tl + base:tl + HIST_PAD, :]
            hist_ref[s, :, _lanes(lb)] = new_hist
            uext_ref[s, lb, base:HIST_PAD, :] = new_hist

    cos = cos_ref[...]
    sin = sin_ref[...]
    q0, k0, v0, g0 = (i * D_RET for i in range(4))
    for s in range(nb):
        rows = slice(s * tl, (s + 1) * tl)
        for h in range(N_RET_HEADS):
            hs = slice(h * HEAD_DIM, (h + 1) * HEAD_DIM)
            lane = lambda off: slice(off + h * HEAD_DIM, off + (h + 1) * HEAD_DIM)
            q = p[rows, lane(q0)]
            k = p[rows, lane(k0)]
            v = p[rows, lane(v0)].astype(BF16)
            gate = p[rows, lane(g0)]
            qr = (q * cos + pltpu.roll(q, HEAD_DIM // 2, 1) * sin).astype(BF16)
            kr = (k * cos + pltpu.roll(k, HEAD_DIM // 2, 1) * sin) * (HEAD_DIM ** -0.5)
            state = rstate_ref[s, h]
            g_chunk = math.exp(log_gamma[h] * chunk)
            for ci in range(tl // chunk):
                cs = slice(ci * chunk, (ci + 1) * chunk)
                qc, kc, vc = qr[cs], kr[cs], v[cs]
                sc = lax.dot_general(qc, kc.astype(BF16), (((1,), (1,)), ((), ())),
                                     preferred_element_type=F32) * dmask_ref[h]
                o = (jnp.dot(sc.astype(BF16), vc, preferred_element_type=F32)
                     + jnp.dot(qc, state.astype(BF16), preferred_element_type=F32) * decay_ref[h, 0])
                kz = (kc * decay_ref[h, 1]).astype(BF16)
                state = state * g_chunk + lax.dot_general(
                    kz, vc, (((0,), (0,)), ((), ())), preferred_element_type=F32)
                mu = jnp.mean(o, axis=-1, keepdims=True)
                do = o - mu
                var = jnp.mean(do * do, axis=-1, keepdims=True)
                on = do * lax.rsqrt(var + EPS)
                r = _silu(gate[cs]) * (on * gng_ref[:, hs])
                mix_ref[s * tl + ci * chunk:s * tl + (ci + 1) * chunk,
                        D_CONV + h * HEAD_DIM:D_CONV + (h + 1) * HEAD_DIM] = r.astype(BF16)
            rstate_ref[s, h] = state
            rout_ref[s, h] = state

    y = []
    for rows in _row_blocks(m):
        out = jnp.dot(mix_ref[rows, :], wout_ref[...], preferred_element_type=F32)
        y.append(x[rows] + _rmsnorm(out, gpost_ref[...]))
    y_ref[...] = jnp.concatenate(y, axis=0).reshape(nb, tl, D_MODEL)


def _mixer(x, cos, sin, hist0, r0, gpre, win, cw, cb, lng, lnb, gng, wout, gpost, *, nb, tl,
           cast):
    bsz, seq, _ = x.shape
    assert bsz % nb == 0 and seq % tl == 0 and tl % V7X_SUBLANES == 0
    chunk = min(RET_CHUNK, tl)
    assert tl % chunk == 0
    m = nb * tl
    wsteps = MIX_WEIGHT_STEPS if cast else 0
    tile = lambda t: jnp.maximum(t - wsteps, 0)
    weights = (D_MODEL * D_IN + D_MODEL * D_MODEL) * 2
    scratch = (nb * (tl + HIST_PAD) * D_CONV * 4 + nb * N_RET_HEADS * HEAD_DIM * HEAD_DIM * 4
               + m * D_MODEL * 2 + N_RET_HEADS * chunk * (chunk + 2 * HEAD_DIM) * 4)
    out_specs = [
        pl.BlockSpec((nb, tl, D_MODEL), lambda b, t: (b, tile(t), 0)),
        pl.BlockSpec((nb, HIST, D_CONV), lambda b, t: (b, 0, 0)),
        pl.BlockSpec((nb, N_RET_HEADS, HEAD_DIM, HEAD_DIM), lambda b, t: (b, 0, 0, 0)),
    ]
    out_shape = [
        jax.ShapeDtypeStruct((bsz, seq, D_MODEL), F32),
        jax.ShapeDtypeStruct((bsz, HIST, D_CONV), F32),
        jax.ShapeDtypeStruct((bsz, N_RET_HEADS, HEAD_DIM, HEAD_DIM), F32),
    ]
    scratch_shapes = [
        pltpu.VMEM((nb, LANE_BLOCKS, tl + HIST_PAD, V7X_LANES), F32),
        pltpu.VMEM((nb, N_RET_HEADS, HEAD_DIM, HEAD_DIM), F32),
        pltpu.VMEM((m, D_MODEL), BF16),
        pltpu.VMEM((N_RET_HEADS, chunk, chunk), F32),
        pltpu.VMEM((N_RET_HEADS, 2, chunk, HEAD_DIM), F32),
    ]
    if cast:
        assert bsz == nb, "the weight steps run once, on the only sequence group"
        rows = D_MODEL // MIX_WEIGHT_STEPS
        row_block = lambda b, t: (jnp.minimum(t, MIX_WEIGHT_STEPS - 1), 0)
        win_spec = pl.BlockSpec((rows, D_IN), row_block)
        wout_spec = pl.BlockSpec((rows, D_MODEL), row_block)
        out_specs += [win_spec, wout_spec]
        out_shape += [jax.ShapeDtypeStruct((D_MODEL, D_IN), BF16),
                      jax.ShapeDtypeStruct((D_MODEL, D_MODEL), BF16)]
        scratch_shapes += [pltpu.VMEM((D_MODEL, D_IN), BF16), pltpu.VMEM((D_MODEL, D_MODEL), BF16)]
        weights += 2 * rows * (D_IN + D_MODEL) * 6
        body = _mixer_cast_kernel
    else:
        win_spec, wout_spec = _resident((D_MODEL, D_IN)), _resident((D_MODEL, D_MODEL))
        body = _mixer_kernel
    return pl.pallas_call(
        functools.partial(body, nb=nb, tl=tl, chunk=chunk),
        grid=(bsz // nb, wsteps + seq // tl),
        in_specs=[
            pl.BlockSpec((nb, tl, D_MODEL), lambda b, t: (b, tile(t), 0)),
            pl.BlockSpec((tl, HEAD_DIM), lambda b, t: (tile(t), 0)),
            pl.BlockSpec((tl, HEAD_DIM), lambda b, t: (tile(t), 0)),
            pl.BlockSpec((nb, HIST, D_CONV), lambda b, t: (b, 0, 0)),
            pl.BlockSpec((nb, N_RET_HEADS, HEAD_DIM, HEAD_DIM), lambda b, t: (b, 0, 0, 0)),
            _resident((1, D_MODEL)), win_spec, _resident((CONV_WIDTH, D_CONV)),
            _resident((1, D_CONV)), _resident((1, D_CONV)), _resident((1, D_CONV)),
            _resident((1, D_RET)), wout_spec, _resident((1, D_MODEL)),
        ],
        out_specs=out_specs,
        out_shape=out_shape,
        scratch_shapes=scratch_shapes,
        compiler_params=pltpu.CompilerParams(
            dimension_semantics=("arbitrary", "arbitrary"),
            vmem_limit_bytes=_vmem_limit(weights + scratch, 2 * m * D_MODEL * 4,
                                         m * (D_IN + 3 * D_MODEL) * 4)),
        name="mixer",
    )(x, cos, sin, hist0, r0, gpre, win, cw, cb, lng, lnb, gng, wout, gpost)


def _rope_tables(pos0, n):
    half = HEAD_DIM // 2
    inv = ROPE_THETA ** (-np.arange(half, dtype=np.float64) / half)
    ang = (pos0 + np.arange(n, dtype=np.float64))[:, None] * inv[None, :]
    cos, sin = np.cos(ang), np.sin(ang)
    return (np.concatenate([cos, cos], axis=-1).astype(np.float32),
            np.concatenate([-sin, sin], axis=-1).astype(np.float32))


def _ffn_both(xp, xs, ffn_w):
    yp, ys = _ffn(xp.reshape(-1, D_MODEL), xs.reshape(-1, D_MODEL), *ffn_w)
    return yp.reshape(xp.shape), ys.reshape(xs.shape)


def kernel(x_prompt, x_sample, state_conv, state_ret, ffn1_norm_pre, ffn1_w_gate, ffn1_w_up, ffn1_w_down, ffn1_norm_post, mix_norm_pre, w_in, conv_w, conv_b, conv_ln_g, conv_ln_b, ret_gn_g, w_out, mix_norm_post, ffn2_norm_pre, ffn2_w_gate, ffn2_w_up, ffn2_w_down, ffn2_norm_post):
    depth = ffn1_norm_pre.shape[0]
    bp, sp, _ = x_prompt.shape
    bs, ss, _ = x_sample.shape
    row = lambda a: a.reshape(1, -1)
    yp, ys = x_prompt, x_sample
    conv_p, ret_p, conv_s, ret_s = [], [], [], []
    for l in range(depth):
        ffn1_w = (row(ffn1_norm_pre[l]), ffn1_w_gate[l], ffn1_w_up[l], ffn1_w_down[l],
                  row(ffn1_norm_post[l]))
        ffn2_w = (row(ffn2_norm_pre[l]), ffn2_w_gate[l], ffn2_w_up[l], ffn2_w_down[l],
                  row(ffn2_norm_post[l]))
        mix_w = [row(mix_norm_pre[l]), w_in[l], conv_w[l], row(conv_b[l]), row(conv_ln_g[l]),
                 row(conv_ln_b[l]), row(ret_gn_g[l]), w_out[l], row(mix_norm_post[l])]
        hist0 = jnp.zeros((bp, HIST, D_CONV), F32)
        r0 = jnp.zeros((bp, N_RET_HEADS, HEAD_DIM, HEAD_DIM), F32)
        yp, ys = _ffn_both(yp, ys, ffn1_w)
        yp, hp, rp, win_bf16, wout_bf16 = _mixer(yp, *_rope_tables(0, sp), hist0, r0, *mix_w,
                                                 nb=bp, tl=min(MIX_TILE, sp), cast=True)
        mix_w[1], mix_w[7] = win_bf16, wout_bf16
        ys, hs, rs = _mixer(ys, *_rope_tables(PAST_LEN, ss), state_conv[l], state_ret[l], *mix_w,
                            nb=bs, tl=ss, cast=False)
        yp, ys = _ffn_both(yp, ys, ffn2_w)
        conv_p.append(hp); ret_p.append(rp); conv_s.append(hs); ret_s.append(rs)
    return (yp, ys, jnp.stack(conv_p), jnp.stack(ret_p), jnp.stack(conv_s), jnp.stack(ret_s))
```

```python
import functools
import math

import numpy as np

import jax
import jax.numpy as jnp
from jax import lax
from jax.experimental import pallas as pl
from jax.experimental.pallas import tpu as pltpu

D_MODEL = 1024
D_FF = 2816
D_CONV = 512
D_RET = 512
N_RET_HEADS = 4
HEAD_DIM = 128
CONV_WIDTH = 31
HIST = CONV_WIDTH - 1
ROPE_THETA = 10000.0
EPS = 1e-6
D_IN = 2 * D_CONV + 4 * D_RET
PAST_LEN = 4096

V7X_SUBLANES = 8
V7X_LANES = 128
LANE_BLOCKS = D_CONV // V7X_LANES
V7X_VMEM_BYTES = 64 * 1024 * 1024

HIST_PAD = 32
CONV_ROW_BLOCK = 64
FFN_TILE = 1024
MIX_TILE = 512
RET_CHUNK = 256
MIX_WEIGHT_STEPS = 4
MIX_EPILOGUE_ROWS = 256
FFN_EPILOGUE_ROWS = 256
V7X_MXU_DIM = 256
FF_CHUNKS = D_FF // V7X_MXU_DIM
assert FF_CHUNKS * V7X_MXU_DIM == D_FF

BF16 = jnp.bfloat16
F32 = jnp.float32


def _vmem_limit(resident_bytes, tile_bytes, temp_bytes):
    need = resident_bytes + 2 * tile_bytes + temp_bytes
    return int(min(need * 5 // 4 + (4 << 20), V7X_VMEM_BYTES - (6 << 20)))


def _rmsnorm(x, g):
    ms = jnp.mean(x * x, axis=-1, keepdims=True)
    return x * lax.rsqrt(ms + EPS) * g


def _lanes(block):
    return slice(block * V7X_LANES, (block + 1) * V7X_LANES)


def _silu(x):
    return x * jax.nn.sigmoid(x)


def _resident(shape):
    nd = len(shape)
    return pl.BlockSpec(shape, lambda *_: (0,) * nd, pipeline_mode=pl.Buffered(1))


def _ffn_chunk(hb, wg_ref, wu_ref, wd_ref, c):
    g = jnp.dot(hb, wg_ref[c], preferred_element_type=F32)
    u = jnp.dot(hb, wu_ref[c], preferred_element_type=F32)
    a = (_silu(g) * u).astype(BF16)
    return jnp.dot(a, wd_ref[c], preferred_element_type=F32)


def _row_blocks(rows, block):
    rb = min(block, rows)
    assert rows % rb == 0
    return [slice(r, r + rb) for r in range(0, rows, rb)]


def _ffn_math(x, gpre_ref, wg_ref, wu_ref, wd_ref, gpost_ref):
    hb = _rmsnorm(x, gpre_ref[...]).astype(BF16)
    acts = []
    for c in range(FF_CHUNKS):
        g = jnp.dot(hb, wg_ref[c], preferred_element_type=F32)
        u = jnp.dot(hb, wu_ref[c], preferred_element_type=F32)
        acts.append((_silu(g) * u).astype(BF16))
    a = jnp.concatenate(acts, axis=1)
    wd = wd_ref[...].reshape(D_FF, D_MODEL)
    out = []
    for rows in _row_blocks(x.shape[0], FFN_EPILOGUE_ROWS):
        d = jnp.dot(a[rows], wd, preferred_element_type=F32)
        out.append(x[rows] + 0.5 * _rmsnorm(d, gpost_ref[...]))
    return jnp.concatenate(out, axis=0)


def _ffn_kernel(xp_ref, xs_ref, gpre_ref, wg32_ref, wu32_ref, wd32_ref, gpost_ref,
                yp_ref, ys_ref, wg_ref, wu_ref, wd_ref, hb_ref, acc_ref, *, n_tiles):
    step = pl.program_id(0)
    last = FF_CHUNKS - 1

    @pl.when(step <= last)
    def _():
        @pl.when(step == 0)
        def _():
            hb_ref[...] = _rmsnorm(xp_ref[...], gpre_ref[...]).astype(BF16)
            acc_ref[...] = jnp.zeros_like(acc_ref)

        wg_ref[step] = wg32_ref[...].astype(BF16)
        wu_ref[step] = wu32_ref[...].astype(BF16)
        wd_ref[step] = wd32_ref[...].astype(BF16)
        acc_ref[...] += _ffn_chunk(hb_ref[...], wg_ref, wu_ref, wd_ref, step)

        @pl.when(step == last)
        def _():
            yp_ref[...] = xp_ref[...] + 0.5 * _rmsnorm(acc_ref[...], gpost_ref[...])

    @pl.when(jnp.logical_and(step > last, step < last + n_tiles))
    def _():
        yp_ref[...] = _ffn_math(xp_ref[...], gpre_ref, wg_ref, wu_ref, wd_ref, gpost_ref)

    @pl.when(step == last + n_tiles)
    def _():
        ys_ref[...] = _ffn_math(xs_ref[...], gpre_ref, wg_ref, wu_ref, wd_ref, gpost_ref)


def _ffn(xp2d, xs2d, gpre, wg32, wu32, wd32, gpost):
    rows, sample_rows = xp2d.shape[0], xs2d.shape[0]
    tm = FFN_TILE
    assert rows % tm == 0 and sample_rows % V7X_SUBLANES == 0
    n_tiles = rows // tm
    last = FF_CHUNKS - 1
    tile_index = lambda s: (jnp.clip(s - last, 0, n_tiles - 1), 0)
    chunk_index = lambda s: jnp.minimum(s, last)
    prompt_tile = pl.BlockSpec((tm, D_MODEL), tile_index)
    sample_tile = pl.BlockSpec((sample_rows, D_MODEL), lambda s: (0, 0))
    w_bytes = 3 * D_MODEL * D_FF
    return pl.pallas_call(
        functools.partial(_ffn_kernel, n_tiles=n_tiles),
        grid=(last + n_tiles + 1,),
        in_specs=[prompt_tile, sample_tile, _resident((1, D_MODEL)),
                  pl.BlockSpec((D_MODEL, V7X_MXU_DIM), lambda s: (0, chunk_index(s))),
                  pl.BlockSpec((D_MODEL, V7X_MXU_DIM), lambda s: (0, chunk_index(s))),
                  pl.BlockSpec((V7X_MXU_DIM, D_MODEL), lambda s: (chunk_index(s), 0)),
                  _resident((1, D_MODEL))],
        out_specs=[prompt_tile, sample_tile],
        out_shape=[jax.ShapeDtypeStruct((rows, D_MODEL), F32),
                   jax.ShapeDtypeStruct((sample_rows, D_MODEL), F32)],
        scratch_shapes=[pltpu.VMEM((FF_CHUNKS, D_MODEL, V7X_MXU_DIM), BF16),
                        pltpu.VMEM((FF_CHUNKS, D_MODEL, V7X_MXU_DIM), BF16),
                        pltpu.VMEM((FF_CHUNKS, V7X_MXU_DIM, D_MODEL), BF16),
                        pltpu.VMEM((tm, D_MODEL), BF16),
                        pltpu.VMEM((tm, D_MODEL), F32)],
        compiler_params=pltpu.CompilerParams(
            dimension_semantics=("arbitrary",),
            vmem_limit_bytes=_vmem_limit(
                2 * w_bytes + 2 * 3 * D_MODEL * V7X_MXU_DIM * 4 + tm * D_MODEL * 6,
                2 * (tm + sample_rows) * D_MODEL * 4,
                tm * (3 * V7X_MXU_DIM + 3 * D_MODEL) * 4)),
        name="ffn",
    )(xp2d, xs2d, gpre, wg32, wu32, wd32, gpost)


def _cast_rows(step, src_ref, out_ref, resident_ref):
    rows = src_ref.shape[0]
    block = src_ref[...].astype(BF16)
    out_ref[...] = block
    resident_ref[pl.ds(pl.multiple_of(step * rows, rows), rows), :] = block


def _mixer_kernel(*refs, nb, tl, chunk):
    _mixer_tile(pl.program_id(1), *refs, nb=nb, tl=tl, chunk=chunk)


def _mixer_cast_kernel(x_ref, cos_ref, sin_ref, hist0_ref, r0_ref, gpre_ref, win32_ref, cw_ref,
                       cb_ref, lng_ref, lnb_ref, gng_ref, wout32_ref, gpost_ref,
                       y_ref, hist_ref, rout_ref, win16_ref, wout16_ref,
                       uext_ref, rstate_ref, mix_ref, dmask_ref, decay_ref, win_ref, wout_ref, *,
                       nb, tl, chunk):
    step = pl.program_id(1)

    @pl.when(step < MIX_WEIGHT_STEPS)
    def _():
        _cast_rows(step, win32_ref, win16_ref, win_ref)
        _cast_rows(step, wout32_ref, wout16_ref, wout_ref)

    @pl.when(step >= MIX_WEIGHT_STEPS)
    def _():
        _mixer_tile(step - MIX_WEIGHT_STEPS, x_ref, cos_ref, sin_ref, hist0_ref, r0_ref, gpre_ref,
                    win_ref, cw_ref, cb_ref, lng_ref, lnb_ref, gng_ref, wout_ref, gpost_ref,
                    y_ref, hist_ref, rout_ref, uext_ref, rstate_ref, mix_ref, dmask_ref,
                    decay_ref, nb=nb, tl=tl, chunk=chunk)


def _mixer_tile(t, x_ref, cos_ref, sin_ref, hist0_ref, r0_ref, gpre_ref, win_ref, cw_ref,
                cb_ref, lng_ref, lnb_ref, gng_ref, wout_ref, gpost_ref,
                y_ref, hist_ref, rout_ref,
                uext_ref, rstate_ref, mix_ref, dmask_ref, decay_ref, *, nb, tl, chunk):
    log_gamma = [math.log1p(-(2.0 ** (-5.0 - h))) for h in range(N_RET_HEADS)]

    @pl.when(t == 0)
    def _init():
        for lb in range(LANE_BLOCKS):
            uext_ref[:, lb, HIST_PAD - HIST:HIST_PAD, :] = hist0_ref[:, :, _lanes(lb)]
        rstate_ref[...] = r0_ref[...]
        ii = lax.broadcasted_iota(jnp.int32, (chunk, chunk), 0)
        jj = lax.broadcasted_iota(jnp.int32, (chunk, chunk), 1)
        rel = (ii - jj).astype(F32)
        row = lax.broadcasted_iota(jnp.int32, (chunk, HEAD_DIM), 0).astype(F32)
        for h in range(N_RET_HEADS):
            dmask_ref[h] = jnp.where(rel >= 0.0, jnp.exp(log_gamma[h] * jnp.maximum(rel, 0.0)), 0.0)
            decay_ref[h, 0] = jnp.exp(log_gamma[h] * (row + 1.0))
            decay_ref[h, 1] = jnp.exp(log_gamma[h] * (chunk - 1.0 - row))

    m = nb * tl
    x = x_ref[...].reshape(m, D_MODEL)
    hb = _rmsnorm(x, gpre_ref[...]).astype(BF16)
    ab = jnp.dot(hb, win_ref[:, :2 * D_CONV], preferred_element_type=F32)
    p = jnp.dot(hb, win_ref[:, 2 * D_CONV:], preferred_element_type=F32)

    u = ab[:, :D_CONV] * jax.nn.sigmoid(ab[:, D_CONV:])
    for s in range(nb):
        for lb in range(LANE_BLOCKS):
            uext_ref[s, lb, HIST_PAD:HIST_PAD + tl, :] = u[s * tl:(s + 1) * tl, _lanes(lb)]
    rb = min(CONV_ROW_BLOCK, tl)
    base = HIST_PAD - HIST
    for s in range(nb):
        for r0 in range(0, tl, rb):
            taps = []
            for lb in range(LANE_BLOCKS):
                acc = jnp.broadcast_to(cb_ref[:, _lanes(lb)], (rb, V7X_LANES))
                for j in range(CONV_WIDTH):
                    acc = acc + (cw_ref[j:j + 1, _lanes(lb)]
                                 * uext_ref[s, lb, r0 + base + j:r0 + base + j + rb, :])
                taps.append(acc)
            acc = jnp.concatenate(taps, axis=-1)
            mu = jnp.mean(acc, axis=-1, keepdims=True)
            dc = acc - mu
            var = jnp.mean(dc * dc, axis=-1, keepdims=True)
            c = _silu(dc * lax.rsqrt(var + EPS) * lng_ref[...] + lnb_ref[...])
            mix_ref[s * tl + r0:s * tl + r0 + rb, 0:D_CONV] = c.astype(BF16)
        for lb in range(LANE_BLOCKS):
            new_hist = uext_ref[s, lb, tl + base:tl + HIST_PAD, :]
            hist_ref[s, :, _lanes(lb)] = new_hist
            uext_ref[s, lb, base:HIST_PAD, :] = new_hist

    cos = cos_ref[...]
    sin = sin_ref[...]
    q0, k0, v0, g0 = (i * D_RET for i in range(4))
    for s in range(nb):
        rows = slice(s * tl, (s + 1) * tl)
        for h in range(N_RET_HEADS):
            hs = slice(h * HEAD_DIM, (h + 1) * HEAD_DIM)
            lane = lambda off: slice(off + h * HEAD_DIM, off + (h + 1) * HEAD_DIM)
            q = p[rows, lane(q0)]
            k = p[rows, lane(k0)]
            v = p[rows, lane(v0)].astype(BF16)
            gate = p[rows, lane(g0)]
            qr = (q * cos + pltpu.roll(q, HEAD_DIM // 2, 1) * sin).astype(BF16)
            kr = (k * cos + pltpu.roll(k, HEAD_DIM // 2, 1) * sin) * (HEAD_DIM ** -0.5)
            state = rstate_ref[s, h]
            g_chunk = math.exp(log_gamma[h] * chunk)
            for ci in range(tl // chunk):
                cs = slice(ci * chunk, (ci + 1) * chunk)
                qc, kc, vc = qr[cs], kr[cs], v[cs]
                sc = lax.dot_general(qc, kc.astype(BF16), (((1,), (1,)), ((), ())),
                                     preferred_element_type=F32) * dmask_ref[h]
                o = (jnp.dot(sc.astype(BF16), vc, preferred_element_type=F32)
                     + jnp.dot(qc, state.astype(BF16), preferred_element_type=F32) * decay_ref[h, 0])
                kz = (kc * decay_ref[h, 1]).astype(BF16)
                state = state * g_chunk + lax.dot_general(
                    kz, vc, (((0,), (0,)), ((), ())), preferred_element_type=F32)
                mu = jnp.mean(o, axis=-1, keepdims=True)
                do = o - mu
                var = jnp.mean(do * do, axis=-1, keepdims=True)
                on = do * lax.rsqrt(var + EPS)
                r = _silu(gate[cs]) * (on * gng_ref[:, hs])
                mix_ref[s * tl + ci * chunk:s * tl + (ci + 1) * chunk,
                        D_CONV + h * HEAD_DIM:D_CONV + (h + 1) * HEAD_DIM] = r.astype(BF16)
            rstate_ref[s, h] = state
            rout_ref[s, h] = state

    y = []
    for rows in _row_blocks(m, MIX_EPILOGUE_ROWS):
        out = jnp.dot(mix_ref[rows, :], wout_ref[...], preferred_element_type=F32)
        y.append(x[rows] + _rmsnorm(out, gpost_ref[...]))
    y_ref[...] = jnp.concatenate(y, axis=0).reshape(nb, tl, D_MODEL)


def _mixer(x, cos, sin, hist0, r0, gpre, win, cw, cb, lng, lnb, gng, wout, gpost, *, nb, tl,
           cast):
    bsz, seq, _ = x.shape
    assert bsz % nb == 0 and seq % tl == 0 and tl % V7X_SUBLANES == 0
    chunk = min(RET_CHUNK, tl)
    assert tl % chunk == 0
    m = nb * tl
    wsteps = MIX_WEIGHT_STEPS if cast else 0
    tile = lambda t: jnp.maximum(t - wsteps, 0)
    weights = (D_MODEL * D_IN + D_MODEL * D_MODEL) * 2
    scratch = (nb * (tl + HIST_PAD) * D_CONV * 4 + nb * N_RET_HEADS * HEAD_DIM * HEAD_DIM * 4
               + m * D_MODEL * 2 + N_RET_HEADS * chunk * (chunk + 2 * HEAD_DIM) * 4)
    out_specs = [
        pl.BlockSpec((nb, tl, D_MODEL), lambda b, t: (b, tile(t), 0)),
        pl.BlockSpec((nb, HIST, D_CONV), lambda b, t: (b, 0, 0)),
        pl.BlockSpec((nb, N_RET_HEADS, HEAD_DIM, HEAD_DIM), lambda b, t: (b, 0, 0, 0)),
    ]
    out_shape = [
        jax.ShapeDtypeStruct((bsz, seq, D_MODEL), F32),
        jax.ShapeDtypeStruct((bsz, HIST, D_CONV), F32),
        jax.ShapeDtypeStruct((bsz, N_RET_HEADS, HEAD_DIM, HEAD_DIM), F32),
    ]
    scratch_shapes = [
        pltpu.VMEM((nb, LANE_BLOCKS, tl + HIST_PAD, V7X_LANES), F32),
        pltpu.VMEM((nb, N_RET_HEADS, HEAD_DIM, HEAD_DIM), F32),
        pltpu.VMEM((m, D_MODEL), BF16),
        pltpu.VMEM((N_RET_HEADS, chunk, chunk), F32),
        pltpu.VMEM((N_RET_HEADS, 2, chunk, HEAD_DIM), F32),
    ]
    if cast:
        assert bsz == nb, "the weight steps run once, on the only sequence group"
        rows = D_MODEL // MIX_WEIGHT_STEPS
        row_block = lambda b, t: (jnp.minimum(t, MIX_WEIGHT_STEPS - 1), 0)
        win_spec = pl.BlockSpec((rows, D_IN), row_block)
        wout_spec = pl.BlockSpec((rows, D_MODEL), row_block)
        out_specs += [win_spec, wout_spec]
        out_shape += [jax.ShapeDtypeStruct((D_MODEL, D_IN), BF16),
                      jax.ShapeDtypeStruct((D_MODEL, D_MODEL), BF16)]
        scratch_shapes += [pltpu.VMEM((D_MODEL, D_IN), BF16), pltpu.VMEM((D_MODEL, D_MODEL), BF16)]
        weights += 2 * rows * (D_IN + D_MODEL) * 6
        body = _mixer_cast_kernel
    else:
        win_spec, wout_spec = _resident((D_MODEL, D_IN)), _resident((D_MODEL, D_MODEL))
        body = _mixer_kernel
    return pl.pallas_call(
        functools.partial(body, nb=nb, tl=tl, chunk=chunk),
        grid=(bsz // nb, wsteps + seq // tl),
        in_specs=[
            pl.BlockSpec((nb, tl, D_MODEL), lambda b, t: (b, tile(t), 0)),
            pl.BlockSpec((tl, HEAD_DIM), lambda b, t: (tile(t), 0)),
            pl.BlockSpec((tl, HEAD_DIM), lambda b, t: (tile(t), 0)),
            pl.BlockSpec((nb, HIST, D_CONV), lambda b, t: (b, 0, 0)),
            pl.BlockSpec((nb, N_RET_HEADS, HEAD_DIM, HEAD_DIM), lambda b, t: (b, 0, 0, 0)),
            _resident((1, D_MODEL)), win_spec, _resident((CONV_WIDTH, D_CONV)),
            _resident((1, D_CONV)), _resident((1, D_CONV)), _resident((1, D_CONV)),
            _resident((1, D_RET)), wout_spec, _resident((1, D_MODEL)),
        ],
        out_specs=out_specs,
        out_shape=out_shape,
        scratch_shapes=scratch_shapes,
        compiler_params=pltpu.CompilerParams(
            dimension_semantics=("arbitrary", "arbitrary"),
            vmem_limit_bytes=_vmem_limit(weights + scratch, 2 * m * D_MODEL * 4,
                                         m * (D_IN + 3 * D_MODEL) * 4)),
        name="mixer",
    )(x, cos, sin, hist0, r0, gpre, win, cw, cb, lng, lnb, gng, wout, gpost)


def _rope_tables(pos0, n):
    half = HEAD_DIM // 2
    inv = ROPE_THETA ** (-np.arange(half, dtype=np.float64) / half)
    ang = (pos0 + np.arange(n, dtype=np.float64))[:, None] * inv[None, :]
    cos, sin = np.cos(ang), np.sin(ang)
    return (np.concatenate([cos, cos], axis=-1).astype(np.float32),
            np.concatenate([-sin, sin], axis=-1).astype(np.float32))


def _ffn_both(xp, xs, ffn_w):
    yp, ys = _ffn(xp.reshape(-1, D_MODEL), xs.reshape(-1, D_MODEL), *ffn_w)
    return yp.reshape(xp.shape), ys.reshape(xs.shape)


def kernel(x_prompt, x_sample, state_conv, state_ret, ffn1_norm_pre, ffn1_w_gate, ffn1_w_up, ffn1_w_down, ffn1_norm_post, mix_norm_pre, w_in, conv_w, conv_b, conv_ln_g, conv_ln_b, ret_gn_g, w_out, mix_norm_post, ffn2_norm_pre, ffn2_w_gate, ffn2_w_up, ffn2_w_down, ffn2_norm_post):
    depth = ffn1_norm_pre.shape[0]
    bp, sp, _ = x_prompt.shape
    bs, ss, _ = x_sample.shape
    row = lambda a: a.reshape(1, -1)
    yp, ys = x_prompt, x_sample
    conv_p, ret_p, conv_s, ret_s = [], [], [], []
    for l in range(depth):
        ffn1_w = (row(ffn1_norm_pre[l]), ffn1_w_gate[l], ffn1_w_up[l], ffn1_w_down[l],
                  row(ffn1_norm_post[l]))
        ffn2_w = (row(ffn2_norm_pre[l]), ffn2_w_gate[l], ffn2_w_up[l], ffn2_w_down[l],
                  row(ffn2_norm_post[l]))
        mix_w = [row(mix_norm_pre[l]), w_in[l], conv_w[l], row(conv_b[l]), row(conv_ln_g[l]),
                 row(conv_ln_b[l]), row(ret_gn_g[l]), w_out[l], row(mix_norm_post[l])]
        hist0 = jnp.zeros((bp, HIST, D_CONV), F32)
        r0 = jnp.zeros((bp, N_RET_HEADS, HEAD_DIM, HEAD_DIM), F32)
        yp, ys = _ffn_both(yp, ys, ffn1_w)
        yp, hp, rp, win_bf16, wout_bf16 = _mixer(yp, *_rope_tables(0, sp), hist0, r0, *mix_w,
                                                 nb=bp, tl=min(MIX_TILE, sp), cast=True)
        mix_w[1], mix_w[7] = win_bf16, wout_bf16
        ys, hs, rs = _mixer(ys, *_rope_tables(PAST_LEN, ss), state_conv[l], state_ret[l], *mix_w,
                            nb=bs, tl=ss, cast=False)
        yp, ys = _ffn_both(yp, ys, ffn2_w)
        conv_p.append(hp); ret_p.append(rp); conv_s.append(hs); ret_s.append(rs)
    return (yp, ys, jnp.stack(conv_p), jnp.stack(ret_p), jnp.stack(conv_s), jnp.stack(ret_s))
```

```python
import functools
import math

import numpy as np

import jax
import jax.numpy as jnp
from jax import lax
from jax.experimental import pallas as pl
from jax.experimental.pallas import tpu as pltpu

D_MODEL = 1024
D_FF = 2816
D_CONV = 512
D_RET = 512
N_RET_HEADS = 4
HEAD_DIM = 128
CONV_WIDTH = 31
HIST = CONV_WIDTH - 1
ROPE_THETA = 10000.0
EPS = 1e-6
D_IN = 2 * D_CONV + 4 * D_RET
PAST_LEN = 4096

V7X_SUBLANES = 8
V7X_LANES = 128
LANE_BLOCKS = D_CONV // V7X_LANES
V7X_VMEM_BYTES = 64 * 1024 * 1024

HIST_PAD = 32
CONV_ROW_BLOCK = 32
FFN_TILE = 1024
MIX_TILE = 512
RET_CHUNK = 256
MIX_WEIGHT_STEPS = 4
MIX_EPILOGUE_ROWS = 256
FFN_EPILOGUE_ROWS = 256
V7X_MXU_DIM = 256
FF_CHUNKS = D_FF // V7X_MXU_DIM
assert FF_CHUNKS * V7X_MXU_DIM == D_FF

BF16 = jnp.bfloat16
F32 = jnp.float32


def _vmem_limit(resident_bytes, tile_bytes, temp_bytes):
    need = resident_bytes + 2 * tile_bytes + temp_bytes
    return int(min(need * 5 // 4 + (4 << 20), V7X_VMEM_BYTES - (6 << 20)))


def _rmsnorm(x, g):
    ms = jnp.mean(x * x, axis=-1, keepdims=True)
    return x * lax.rsqrt(ms + EPS) * g


def _lanes(block):
    return slice(block * V7X_LANES, (block + 1) * V7X_LANES)


def _silu(x):
    return x * jax.nn.sigmoid(x)


def _resident(shape):
    nd = len(shape)
    return pl.BlockSpec(shape, lambda *_: (0,) * nd, pipeline_mode=pl.Buffered(1))


def _ffn_chunk(hb, wg_ref, wu_ref, wd_ref, c):
    g = jnp.dot(hb, wg_ref[c], preferred_element_type=F32)
    u = jnp.dot(hb, wu_ref[c], preferred_element_type=F32)
    a = (_silu(g) * u).astype(BF16)
    return jnp.dot(a, wd_ref[c], preferred_element_type=F32)


def _row_blocks(rows, block):
    rb = min(block, rows)
    assert rows % rb == 0
    return [slice(r, r + rb) for r in range(0, rows, rb)]


def _ffn_math(x, gpre_ref, wg_ref, wu_ref, wd_ref, gpost_ref):
    hb = _rmsnorm(x, gpre_ref[...]).astype(BF16)
    acts = []
    for c in range(FF_CHUNKS):
        g = jnp.dot(hb, wg_ref[c], preferred_element_type=F32)
        u = jnp.dot(hb, wu_ref[c], preferred_element_type=F32)
        acts.append((_silu(g) * u).astype(BF16))
    a = jnp.concatenate(acts, axis=1)
    wd = wd_ref[...].reshape(D_FF, D_MODEL)
    out = []
    for rows in _row_blocks(x.shape[0], FFN_EPILOGUE_ROWS):
        d = jnp.dot(a[rows], wd, preferred_element_type=F32)
        out.append(x[rows] + 0.5 * _rmsnorm(d, gpost_ref[...]))
    return jnp.concatenate(out, axis=0)


def _ffn_kernel(xp_ref, xs_ref, gpre_ref, wg32_ref, wu32_ref, wd32_ref, gpost_ref,
                yp_ref, ys_ref, wg_ref, wu_ref, wd_ref, hb_ref, acc_ref, *, n_tiles):
    step = pl.program_id(0)
    last = FF_CHUNKS - 1

    @pl.when(step <= last)
    def _():
        @pl.when(step == 0)
        def _():
            hb_ref[...] = _rmsnorm(xp_ref[...], gpre_ref[...]).astype(BF16)
            acc_ref[...] = jnp.zeros_like(acc_ref)

        wg_ref[step] = wg32_ref[...].astype(BF16)
        wu_ref[step] = wu32_ref[...].astype(BF16)
        wd_ref[step] = wd32_ref[...].astype(BF16)
        acc_ref[...] += _ffn_chunk(hb_ref[...], wg_ref, wu_ref, wd_ref, step)

        @pl.when(step == last)
        def _():
            yp_ref[...] = xp_ref[...] + 0.5 * _rmsnorm(acc_ref[...], gpost_ref[...])

    @pl.when(jnp.logical_and(step > last, step < last + n_tiles))
    def _():
        yp_ref[...] = _ffn_math(xp_ref[...], gpre_ref, wg_ref, wu_ref, wd_ref, gpost_ref)

    @pl.when(step == last + n_tiles)
    def _():
        ys_ref[...] = _ffn_math(xs_ref[...], gpre_ref, wg_ref, wu_ref, wd_ref, gpost_ref)


def _ffn(xp2d, xs2d, gpre, wg32, wu32, wd32, gpost):
    rows, sample_rows = xp2d.shape[0], xs2d.shape[0]
    tm = FFN_TILE
    assert rows % tm == 0 and sample_rows % V7X_SUBLANES == 0
    n_tiles = rows // tm
    last = FF_CHUNKS - 1
    tile_index = lambda s: (jnp.clip(s - last, 0, n_tiles - 1), 0)
    chunk_index = lambda s: jnp.minimum(s, last)
    prompt_tile = pl.BlockSpec((tm, D_MODEL), tile_index)
    sample_tile = pl.BlockSpec((sample_rows, D_MODEL), lambda s: (0, 0))
    w_bytes = 3 * D_MODEL * D_FF
    return pl.pallas_call(
        functools.partial(_ffn_kernel, n_tiles=n_tiles),
        grid=(last + n_tiles + 1,),
        in_specs=[prompt_tile, sample_tile, _resident((1, D_MODEL)),
                  pl.BlockSpec((D_MODEL, V7X_MXU_DIM), lambda s: (0, chunk_index(s))),
                  pl.BlockSpec((D_MODEL, V7X_MXU_DIM), lambda s: (0, chunk_index(s))),
                  pl.BlockSpec((V7X_MXU_DIM, D_MODEL), lambda s: (chunk_index(s), 0)),
                  _resident((1, D_MODEL))],
        out_specs=[prompt_tile, sample_tile],
        out_shape=[jax.ShapeDtypeStruct((rows, D_MODEL), F32),
                   jax.ShapeDtypeStruct((sample_rows, D_MODEL), F32)],
        scratch_shapes=[pltpu.VMEM((FF_CHUNKS, D_MODEL, V7X_MXU_DIM), BF16),
                        pltpu.VMEM((FF_CHUNKS, D_MODEL, V7X_MXU_DIM), BF16),
                        pltpu.VMEM((FF_CHUNKS, V7X_MXU_DIM, D_MODEL), BF16),
                        pltpu.VMEM((tm, D_MODEL), BF16),
                        pltpu.VMEM((tm, D_MODEL), F32)],
        compiler_params=pltpu.CompilerParams(
            dimension_semantics=("arbitrary",),
            vmem_limit_bytes=_vmem_limit(
                2 * w_bytes + 2 * 3 * D_MODEL * V7X_MXU_DIM * 4 + tm * D_MODEL * 6,
                2 * (tm + sample_rows) * D_MODEL * 4,
                tm * (3 * V7X_MXU_DIM + 3 * D_MODEL) * 4)),
        name="ffn",
    )(xp2d, xs2d, gpre, wg32, wu32, wd32, gpost)


def _cast_rows(step, src_ref, out_ref, resident_ref):
    rows = src_ref.shape[0]
    block = src_ref[...].astype(BF16)
    out_ref[...] = block
    resident_ref[pl.ds(pl.multiple_of(step * rows, rows), rows), :] = block


def _mixer_kernel(*refs, nb, tl, chunk):
    _mixer_tile(pl.program_id(1), *refs, nb=nb, tl=tl, chunk=chunk)


def _mixer_cast_kernel(x_ref, cos_ref, sin_ref, hist0_ref, r0_ref, gpre_ref, win32_ref, cw_ref,
                       cb_ref, lng_ref, lnb_ref, gng_ref, wout32_ref, gpost_ref,
                       y_ref, hist_ref, rout_ref, win16_ref, wout16_ref,
                       uext_ref, rstate_ref, mix_ref, dmask_ref, decay_ref, win_ref, wout_ref, *,
                       nb, tl, chunk):
    step = pl.program_id(1)

    @pl.when(step < MIX_WEIGHT_STEPS)
    def _():
        _cast_rows(step, win32_ref, win16_ref, win_ref)
        _cast_rows(step, wout32_ref, wout16_ref, wout_ref)

    @pl.when(step >= MIX_WEIGHT_STEPS)
    def _():
        _mixer_tile(step - MIX_WEIGHT_STEPS, x_ref, cos_ref, sin_ref, hist0_ref, r0_ref, gpre_ref,
                    win_ref, cw_ref, cb_ref, lng_ref, lnb_ref, gng_ref, wout_ref, gpost_ref,
                    y_ref, hist_ref, rout_ref, uext_ref, rstate_ref, mix_ref, dmask_ref,
                    decay_ref, nb=nb, tl=tl, chunk=chunk)


def _mixer_tile(t, x_ref, cos_ref, sin_ref, hist0_ref, r0_ref, gpre_ref, win_ref, cw_ref,
                cb_ref, lng_ref, lnb_ref, gng_ref, wout_ref, gpost_ref,
                y_ref, hist_ref, rout_ref,
                uext_ref, rstate_ref, mix_ref, dmask_ref, decay_ref, *, nb, tl, chunk):
    log_gamma = [math.log1p(-(2.0 ** (-5.0 - h))) for h in range(N_RET_HEADS)]

    @pl.when(t == 0)
    def _init():
        for lb in range(LANE_BLOCKS):
            uext_ref[:, lb, HIST_PAD - HIST:HIST_PAD, :] = hist0_ref[:, :, _lanes(lb)]
        rstate_ref[...] = r0_ref[...]
        ii = lax.broadcasted_iota(jnp.int32, (chunk, chunk), 0)
        jj = lax.broadcasted_iota(jnp.int32, (chunk, chunk), 1)
        rel = (ii - jj).astype(F32)
        row = lax.broadcasted_iota(jnp.int32, (chunk, HEAD_DIM), 0).astype(F32)
        for h in range(N_RET_HEADS):
            dmask_ref[h] = jnp.where(rel >= 0.0, jnp.exp(log_gamma[h] * jnp.maximum(rel, 0.0)), 0.0)
            decay_ref[h, 0] = jnp.exp(log_gamma[h] * (row + 1.0))
            decay_ref[h, 1] = jnp.exp(log_gamma[h] * (chunk - 1.0 - row))

    m = nb * tl
    x = x_ref[...].reshape(m, D_MODEL)
    hb = _rmsnorm(x, gpre_ref[...]).astype(BF16)
    ab = jnp.dot(hb, win_ref[:, :2 * D_CONV], preferred_element_type=F32)
    p = jnp.dot(hb, win_ref[:, 2 * D_CONV:], preferred_element_type=F32)

    u = ab[:, :D_CONV] * jax.nn.sigmoid(ab[:, D_CONV:])
    for s in range(nb):
        for lb in range(LANE_BLOCKS):
            uext_ref[s, lb, HIST_PAD:HIST_PAD + tl, :] = u[s * tl:(s + 1) * tl, _lanes(lb)]
    rb = min(CONV_ROW_BLOCK, tl)
    base = HIST_PAD - HIST
    for s in range(nb):
        for r0 in range(0, tl, rb):
            taps = []
            for lb in range(LANE_BLOCKS):
                acc = jnp.broadcast_to(cb_ref[:, _lanes(lb)], (rb, V7X_LANES))
                for j in range(CONV_WIDTH):
                    acc = acc + (cw_ref[j:j + 1, _lanes(lb)]
                                 * uext_ref[s, lb, r0 + base + j:r0 + base + j + rb, :])
                taps.append(acc)
            acc = jnp.concatenate(taps, axis=-1)
            mu = jnp.mean(acc, axis=-1, keepdims=True)
            dc = acc - mu
            var = jnp.mean(dc * dc, axis=-1, keepdims=True)
            c = _silu(dc * lax.rsqrt(var + EPS) * lng_ref[...] + lnb_ref[...])
            mix_ref[s * tl + r0:s * tl + r0 + rb, 0:D_CONV] = c.astype(BF16)
        for lb in range(LANE_BLOCKS):
            new_hist = uext_ref[s, lb, tl + base:tl + HIST_PAD, :]
            hist_ref[s, :, _lanes(lb)] = new_hist
            uext_ref[s, lb, base:HIST_PAD, :] = new_hist

    cos = cos_ref[...]
    sin = sin_ref[...]
    q0, k0, v0, g0 = (i * D_RET for i in range(4))
    for s in range(nb):
        rows = slice(s * tl, (s + 1) * tl)
        for h in range(N_RET_HEADS):
            hs = slice(h * HEAD_DIM, (h + 1) * HEAD_DIM)
            lane = lambda off: slice(off + h * HEAD_DIM, off + (h + 1) * HEAD_DIM)
            q = p[rows, lane(q0)]
            k = p[rows, lane(k0)]
            v = p[rows, lane(v0)].astype(BF16)
            gate = p[rows, lane(g0)]
            qr = (q * cos + pltpu.roll(q, HEAD_DIM // 2, 1) * sin).astype(BF16)
            kr = (k * cos + pltpu.roll(k, HEAD_DIM // 2, 1) * sin) * (HEAD_DIM ** -0.5)
            state = rstate_ref[s, h]
            g_chunk = math.exp(log_gamma[h] * chunk)
            for ci in range(tl // chunk):
                cs = slice(ci * chunk, (ci + 1) * chunk)
                qc, kc, vc = qr[cs], kr[cs], v[cs]
                sc = lax.dot_general(qc, kc.astype(BF16), (((1,), (1,)), ((), ())),
                                     preferred_element_type=F32) * dmask_ref[h]
                o = (jnp.dot(sc.astype(BF16), vc, preferred_element_type=F32)
                     + jnp.dot(qc, state.astype(BF16), preferred_element_type=F32) * decay_ref[h, 0])
                kz = (kc * decay_ref[h, 1]).astype(BF16)
                state = state * g_chunk + lax.dot_general(
                    kz, vc, (((0,), (0,)), ((), ())), preferred_element_type=F32)
                mu = jnp.mean(o, axis=-1, keepdims=True)
                do = o - mu
                var = jnp.mean(do * do, axis=-1, keepdims=True)
                on = do * lax.rsqrt(var + EPS)
                r = _silu(gate[cs]) * (on * gng_ref[:, hs])
                mix_ref[s * tl + ci * chunk:s * tl + (ci + 1) * chunk,
                        D_CONV + h * HEAD_DIM:D_CONV + (h + 1) * HEAD_DIM] = r.astype(BF16)
            rstate_ref[s, h] = state
            rout_ref[s, h] = state

    y = []
    for rows in _row_blocks(m, MIX_EPILOGUE_ROWS):
        out = jnp.dot(mix_ref[rows, :], wout_ref[...], preferred_element_type=F32)
        y.append(x[rows] + _rmsnorm(out, gpost_ref[...]))
    y_ref[...] = jnp.concatenate(y, axis=0).reshape(nb, tl, D_MODEL)


def _mixer(x, cos, sin, hist0, r0, gpre, win, cw, cb, lng, lnb, gng, wout, gpost, *, nb, tl,
           cast):
    bsz, seq, _ = x.shape
    assert bsz % nb == 0 and seq % tl == 0 and tl % V7X_SUBLANES == 0
    chunk = min(RET_CHUNK, tl)
    assert tl % chunk == 0
    m = nb * tl
    wsteps = MIX_WEIGHT_STEPS if cast else 0
    tile = lambda t: jnp.maximum(t - wsteps, 0)
    weights = (D_MODEL * D_IN + D_MODEL * D_MODEL) * 2
    scratch = (nb * (tl + HIST_PAD) * D_CONV * 4 + nb * N_RET_HEADS * HEAD_DIM * HEAD_DIM * 4
               + m * D_MODEL * 2 + N_RET_HEADS * chunk * (chunk + 2 * HEAD_DIM) * 4)
    out_specs = [
        pl.BlockSpec((nb, tl, D_MODEL), lambda b, t: (b, tile(t), 0)),
        pl.BlockSpec((nb, HIST, D_CONV), lambda b, t: (b, 0, 0)),
        pl.BlockSpec((nb, N_RET_HEADS, HEAD_DIM, HEAD_DIM), lambda b, t: (b, 0, 0, 0)),
    ]
    out_shape = [
        jax.ShapeDtypeStruct((bsz, seq, D_MODEL), F32),
        jax.ShapeDtypeStruct((bsz, HIST, D_CONV), F32),
        jax.ShapeDtypeStruct((bsz, N_RET_HEADS, HEAD_DIM, HEAD_DIM), F32),
    ]
    scratch_shapes = [
        pltpu.VMEM((nb, LANE_BLOCKS, tl + HIST_PAD, V7X_LANES), F32),
        pltpu.VMEM((nb, N_RET_HEADS, HEAD_DIM, HEAD_DIM), F32),
        pltpu.VMEM((m, D_MODEL), BF16),
        pltpu.VMEM((N_RET_HEADS, chunk, chunk), F32),
        pltpu.VMEM((N_RET_HEADS, 2, chunk, HEAD_DIM), F32),
    ]
    if cast:
        assert bsz == nb, "the weight steps run once, on the only sequence group"
        rows = D_MODEL // MIX_WEIGHT_STEPS
        row_block = lambda b, t: (jnp.minimum(t, MIX_WEIGHT_STEPS - 1), 0)
        win_spec = pl.BlockSpec((rows, D_IN), row_block)
        wout_spec = pl.BlockSpec((rows, D_MODEL), row_block)
        out_specs += [win_spec, wout_spec]
        out_shape += [jax.ShapeDtypeStruct((D_MODEL, D_IN), BF16),
                      jax.ShapeDtypeStruct((D_MODEL, D_MODEL), BF16)]
        scratch_shapes += [pltpu.VMEM((D_MODEL, D_IN), BF16), pltpu.VMEM((D_MODEL, D_MODEL), BF16)]
        weights += 2 * rows * (D_IN + D_MODEL) * 6
        body = _mixer_cast_kernel
    else:
        win_spec, wout_spec = _resident((D_MODEL, D_IN)), _resident((D_MODEL, D_MODEL))
        body = _mixer_kernel
    return pl.pallas_call(
        functools.partial(body, nb=nb, tl=tl, chunk=chunk),
        grid=(bsz // nb, wsteps + seq // tl),
        in_specs=[
            pl.BlockSpec((nb, tl, D_MODEL), lambda b, t: (b, tile(t), 0)),
            pl.BlockSpec((tl, HEAD_DIM), lambda b, t: (tile(t), 0)),
            pl.BlockSpec((tl, HEAD_DIM), lambda b, t: (tile(t), 0)),
            pl.BlockSpec((nb, HIST, D_CONV), lambda b, t: (b, 0, 0)),
            pl.BlockSpec((nb, N_RET_HEADS, HEAD_DIM, HEAD_DIM), lambda b, t: (b, 0, 0, 0)),
            _resident((1, D_MODEL)), win_spec, _resident((CONV_WIDTH, D_CONV)),
            _resident((1, D_CONV)), _resident((1, D_CONV)), _resident((1, D_CONV)),
            _resident((1, D_RET)), wout_spec, _resident((1, D_MODEL)),
        ],
        out_specs=out_specs,
        out_shape=out_shape,
        scratch_shapes=scratch_shapes,
        compiler_params=pltpu.CompilerParams(
            dimension_semantics=("arbitrary", "arbitrary"),
            vmem_limit_bytes=_vmem_limit(weights + scratch, 2 * m * D_MODEL * 4,
                                         m * (D_IN + 3 * D_MODEL) * 4)),
        name="mixer",
    )(x, cos, sin, hist0, r0, gpre, win, cw, cb, lng, lnb, gng, wout, gpost)


def _rope_tables(pos0, n):
    half = HEAD_DIM // 2
    inv = ROPE_THETA ** (-np.arange(half, dtype=np.float64) / half)
    ang = (pos0 + np.arange(n, dtype=np.float64))[:, None] * inv[None, :]
    cos, sin = np.cos(ang), np.sin(ang)
    return (np.concatenate([cos, cos], axis=-1).astype(np.float32),
            np.concatenate([-sin, sin], axis=-1).astype(np.float32))


def _ffn_both(xp, xs, ffn_w):
    yp, ys = _ffn(xp.reshape(-1, D_MODEL), xs.reshape(-1, D_MODEL), *ffn_w)
    return yp.reshape(xp.shape), ys.reshape(xs.shape)


def kernel(x_prompt, x_sample, state_conv, state_ret, ffn1_norm_pre, ffn1_w_gate, ffn1_w_up, ffn1_w_down, ffn1_norm_post, mix_norm_pre, w_in, conv_w, conv_b, conv_ln_g, conv_ln_b, ret_gn_g, w_out, mix_norm_post, ffn2_norm_pre, ffn2_w_gate, ffn2_w_up, ffn2_w_down, ffn2_norm_post):
    depth = ffn1_norm_pre.shape[0]
    bp, sp, _ = x_prompt.shape
    bs, ss, _ = x_sample.shape
    row = lambda a: a.reshape(1, -1)
    yp, ys = x_prompt, x_sample
    conv_p, ret_p, conv_s, ret_s = [], [], [], []
    for l in range(depth):
        ffn1_w = (row(ffn1_norm_pre[l]), ffn1_w_gate[l], ffn1_w_up[l], ffn1_w_down[l],
                  row(ffn1_norm_post[l]))
        ffn2_w = (row(ffn2_norm_pre[l]), ffn2_w_gate[l], ffn2_w_up[l], ffn2_w_down[l],
                  row(ffn2_norm_post[l]))
        mix_w = [row(mix_norm_pre[l]), w_in[l], conv_w[l], row(conv_b[l]), row(conv_ln_g[l]),
                 row(conv_ln_b[l]), row(ret_gn_g[l]), w_out[l], row(mix_norm_post[l])]
        hist0 = jnp.zeros((bp, HIST, D_CONV), F32)
        r0 = jnp.zeros((bp, N_RET_HEADS, HEAD_DIM, HEAD_DIM), F32)
        yp, ys = _ffn_both(yp, ys, ffn1_w)
        yp, hp, rp, win_bf16, wout_bf16 = _mixer(yp, *_rope_tables(0, sp), hist0, r0, *mix_w,
                                                 nb=bp, tl=min(MIX_TILE, sp), cast=True)
        mix_w[1], mix_w[7] = win_bf16, wout_bf16
        ys, hs, rs = _mixer(ys, *_rope_tables(PAST_LEN, ss), state_conv[l], state_ret[l], *mix_w,
                            nb=bs, tl=ss, cast=False)
        yp, ys = _ffn_both(yp, ys, ffn2_w)
        conv_p.append(hp); ret_p.append(rp); conv_s.append(hs); ret_s.append(rs)
    return (yp, ys, jnp.stack(conv_p), jnp.stack(ret_p), jnp.stack(conv_s), jnp.stack(ret_s))
```

```python
import functools
import math

import numpy as np

import jax
import jax.numpy as jnp
from jax import lax
from jax.experimental import pallas as pl
from jax.experimental.pallas import tpu as pltpu

D_MODEL = 1024
D_FF = 2816
D_CONV = 512
D_RET = 512
N_RET_HEADS = 4
HEAD_DIM = 128
CONV_WIDTH = 31
HIST = CONV_WIDTH - 1
ROPE_THETA = 10000.0
EPS = 1e-6
D_IN = 2 * D_CONV + 4 * D_RET
PAST_LEN = 4096

V7X_SUBLANES = 8
V7X_LANES = 128
LANE_BLOCKS = D_CONV // V7X_LANES
V7X_VMEM_BYTES = 64 * 1024 * 1024

HIST_PAD = 32
CONV_ROW_BLOCK = 32
FFN_TILE = 1024
MIX_TILE = 512
RET_CHUNK = 256
MIX_WEIGHT_STEPS = 4
MIX_EPILOGUE_ROWS = 256
FFN_EPILOGUE_ROWS = 256
V7X_MXU_DIM = 256
FF_CHUNKS = D_FF // V7X_MXU_DIM
assert FF_CHUNKS * V7X_MXU_DIM == D_FF

BF16 = jnp.bfloat16
F32 = jnp.float32


def _vmem_limit(resident_bytes, tile_bytes, temp_bytes):
    need = resident_bytes + 2 * tile_bytes + temp_bytes
    return int(min(need * 5 // 4 + (4 << 20), V7X_VMEM_BYTES - (1 << 20)))


def _rmsnorm(x, g):
    ms = jnp.mean(x * x, axis=-1, keepdims=True)
    return x * lax.rsqrt(ms + EPS) * g


def _lanes(block):
    return slice(block * V7X_LANES, (block + 1) * V7X_LANES)


def _silu(x):
    return x * jax.nn.sigmoid(x)


def _resident(shape):
    nd = len(shape)
    return pl.BlockSpec(shape, lambda *_: (0,) * nd, pipeline_mode=pl.Buffered(1))


def _ffn_chunk(hb, wg_ref, wu_ref, wd_ref, c):
    g = jnp.dot(hb, wg_ref[c], preferred_element_type=F32)
    u = jnp.dot(hb, wu_ref[c], preferred_element_type=F32)
    a = (_silu(g) * u).astype(BF16)
    return jnp.dot(a, wd_ref[c], preferred_element_type=F32)


def _row_blocks(rows, block):
    rb = min(block, rows)
    assert rows % rb == 0
    return [slice(r, r + rb) for r in range(0, rows, rb)]


def _ffn_math(x, gpre_ref, wg_ref, wu_ref, wd_ref, gpost_ref):
    hb = _rmsnorm(x, gpre_ref[...]).astype(BF16)
    acts = []
    for c in range(FF_CHUNKS):
        g = jnp.dot(hb, wg_ref[c], preferred_element_type=F32)
        u = jnp.dot(hb, wu_ref[c], preferred_element_type=F32)
        acts.append((_silu(g) * u).astype(BF16))
    a = jnp.concatenate(acts, axis=1)
    wd = wd_ref[...].reshape(D_FF, D_MODEL)
    out = []
    for rows in _row_blocks(x.shape[0], FFN_EPILOGUE_ROWS):
        d = jnp.dot(a[rows], wd, preferred_element_type=F32)
        out.append(x[rows] + 0.5 * _rmsnorm(d, gpost_ref[...]))
    return jnp.concatenate(out, axis=0)


def _ffn_kernel(*refs, n_tiles, hosted):
    if hosted:
        (xp_ref, xs_ref, gpre_ref, wg32_ref, wu32_ref, wd32_ref, gpost_ref, win32_ref, wout32_ref,
         yp_ref, ys_ref, win16_ref, wout16_ref, wg_ref, wu_ref, wd_ref, hb_ref) = refs
    else:
        (xp_ref, xs_ref, gpre_ref, wg32_ref, wu32_ref, wd32_ref, gpost_ref,
         yp_ref, ys_ref, wg_ref, wu_ref, wd_ref, hb_ref) = refs
    step = pl.program_id(0)
    last = FF_CHUNKS - 1
    acc_ref = yp_ref

    if hosted:
        @pl.when(step > last)
        def _():
            win16_ref[...] = win32_ref[...].astype(BF16)
            wout16_ref[...] = wout32_ref[...].astype(BF16)

    @pl.when(step <= last)
    def _():
        @pl.when(step == 0)
        def _():
            hb_ref[...] = _rmsnorm(xp_ref[...], gpre_ref[...]).astype(BF16)
            acc_ref[...] = jnp.zeros_like(acc_ref)

        wg_ref[step] = wg32_ref[...].astype(BF16)
        wu_ref[step] = wu32_ref[...].astype(BF16)
        wd_ref[step] = wd32_ref[...].astype(BF16)
        acc_ref[...] += _ffn_chunk(hb_ref[...], wg_ref, wu_ref, wd_ref, step)

        @pl.when(step == last)
        def _():
            yp_ref[...] = xp_ref[...] + 0.5 * _rmsnorm(acc_ref[...], gpost_ref[...])

    @pl.when(jnp.logical_and(step > last, step < last + n_tiles))
    def _():
        yp_ref[...] = _ffn_math(xp_ref[...], gpre_ref, wg_ref, wu_ref, wd_ref, gpost_ref)

    @pl.when(step == last + n_tiles)
    def _():
        ys_ref[...] = _ffn_math(xs_ref[...], gpre_ref, wg_ref, wu_ref, wd_ref, gpost_ref)


def _ffn(xp2d, xs2d, gpre, wg32, wu32, wd32, gpost, mixer_weights=None):
    rows, sample_rows = xp2d.shape[0], xs2d.shape[0]
    tm = FFN_TILE
    assert rows % tm == 0 and sample_rows % V7X_SUBLANES == 0
    n_tiles = rows // tm
    last = FF_CHUNKS - 1
    tile_index = lambda s: (jnp.clip(s - last, 0, n_tiles - 1), 0)
    chunk_index = lambda s: jnp.minimum(s, last)
    prompt_tile = pl.BlockSpec((tm, D_MODEL), tile_index)
    sample_tile = pl.BlockSpec((sample_rows, D_MODEL), lambda s: (0, 0))
    sample_in = pl.BlockSpec((sample_rows, D_MODEL), lambda s: (0, 0), pipeline_mode=pl.Buffered(1))
    w_bytes = 3 * D_MODEL * D_FF
    hosted = mixer_weights is not None
    extra_in, extra_out, extra_shape, extra_args, extra_bytes = [], [], [], [], 0
    if hosted:
        assert D_MODEL % n_tiles == 0
        hrows = D_MODEL // n_tiles
        hblock = lambda s: (jnp.clip(s - FF_CHUNKS, 0, n_tiles - 1), 0)
        extra_in = [pl.BlockSpec((hrows, D_IN), hblock), pl.BlockSpec((hrows, D_MODEL), hblock)]
        extra_out = list(extra_in)
        extra_shape = [jax.ShapeDtypeStruct((D_MODEL, D_IN), BF16),
                       jax.ShapeDtypeStruct((D_MODEL, D_MODEL), BF16)]
        extra_args = list(mixer_weights)
        extra_bytes = 2 * hrows * (D_IN + D_MODEL) * 6
    return pl.pallas_call(
        functools.partial(_ffn_kernel, n_tiles=n_tiles, hosted=hosted),
        grid=(last + n_tiles + 1,),
        in_specs=[prompt_tile, sample_in, _resident((1, D_MODEL)),
                  pl.BlockSpec((D_MODEL, V7X_MXU_DIM), lambda s: (0, chunk_index(s))),
                  pl.BlockSpec((D_MODEL, V7X_MXU_DIM), lambda s: (0, chunk_index(s))),
                  pl.BlockSpec((V7X_MXU_DIM, D_MODEL), lambda s: (chunk_index(s), 0)),
                  _resident((1, D_MODEL))] + extra_in,
        out_specs=[prompt_tile, sample_tile] + extra_out,
        out_shape=[jax.ShapeDtypeStruct((rows, D_MODEL), F32),
                   jax.ShapeDtypeStruct((sample_rows, D_MODEL), F32)] + extra_shape,
        scratch_shapes=[pltpu.VMEM((FF_CHUNKS, D_MODEL, V7X_MXU_DIM), BF16),
                        pltpu.VMEM((FF_CHUNKS, D_MODEL, V7X_MXU_DIM), BF16),
                        pltpu.VMEM((FF_CHUNKS, V7X_MXU_DIM, D_MODEL), BF16),
                        pltpu.VMEM((tm, D_MODEL), BF16)],
        compiler_params=pltpu.CompilerParams(
            dimension_semantics=("arbitrary",),
            vmem_limit_bytes=_vmem_limit(
                2 * w_bytes + 2 * 3 * D_MODEL * V7X_MXU_DIM * 4 + tm * D_MODEL * 6 + extra_bytes,
                2 * (tm + sample_rows) * D_MODEL * 4,
                tm * (3 * V7X_MXU_DIM + 3 * D_MODEL) * 4)),
        name="ffn",
    )(xp2d, xs2d, gpre, wg32, wu32, wd32, gpost, *extra_args)


def _cast_rows(step, src_ref, out_ref, resident_ref):
    rows = src_ref.shape[0]
    block = src_ref[...].astype(BF16)
    out_ref[...] = block
    resident_ref[pl.ds(pl.multiple_of(step * rows, rows), rows), :] = block


def _mixer_kernel(*refs, nb, tl, chunk):
    _mixer_tile(pl.program_id(1), *refs, nb=nb, tl=tl, chunk=chunk)


def _mixer_cast_kernel(x_ref, cos_ref, sin_ref, hist0_ref, r0_ref, gpre_ref, win32_ref, cw_ref,
                       cb_ref, lng_ref, lnb_ref, gng_ref, wout32_ref, gpost_ref,
                       y_ref, hist_ref, rout_ref, win16_ref, wout16_ref,
                       uext_ref, rstate_ref, mix_ref, dmask_ref, decay_ref, win_ref, wout_ref, *,
                       nb, tl, chunk):
    step = pl.program_id(1)

    @pl.when(step < MIX_WEIGHT_STEPS)
    def _():
        _cast_rows(step, win32_ref, win16_ref, win_ref)
        _cast_rows(step, wout32_ref, wout16_ref, wout_ref)

    @pl.when(step >= MIX_WEIGHT_STEPS)
    def _():
        _mixer_tile(step - MIX_WEIGHT_STEPS, x_ref, cos_ref, sin_ref, hist0_ref, r0_ref, gpre_ref,
                    win_ref, cw_ref, cb_ref, lng_ref, lnb_ref, gng_ref, wout_ref, gpost_ref,
                    y_ref, hist_ref, rout_ref, uext_ref, rstate_ref, mix_ref, dmask_ref,
                    decay_ref, nb=nb, tl=tl, chunk=chunk)


def _mixer_tile(t, x_ref, cos_ref, sin_ref, hist0_ref, r0_ref, gpre_ref, win_ref, cw_ref,
                cb_ref, lng_ref, lnb_ref, gng_ref, wout_ref, gpost_ref,
                y_ref, hist_ref, rout_ref,
                uext_ref, rstate_ref, mix_ref, dmask_ref, decay_ref, *, nb, tl, chunk):
    log_gamma = [math.log1p(-(2.0 ** (-5.0 - h))) for h in range(N_RET_HEADS)]

    @pl.when(t == 0)
    def _init():
        for lb in range(LANE_BLOCKS):
            uext_ref[:, lb, HIST_PAD - HIST:HIST_PAD, :] = hist0_ref[:, :, _lanes(lb)]
        rstate_ref[...] = r0_ref[...]
        ii = lax.broadcasted_iota(jnp.int32, (chunk, chunk), 0)
        jj = lax.broadcasted_iota(jnp.int32, (chunk, chunk), 1)
        rel = (ii - jj).astype(F32)
        row = lax.broadcasted_iota(jnp.int32, (chunk, HEAD_DIM), 0).astype(F32)
        for h in range(N_RET_HEADS):
            dmask_ref[h] = jnp.where(rel >= 0.0, jnp.exp(log_gamma[h] * jnp.maximum(rel, 0.0)), 0.0)
            decay_ref[h, 0] = jnp.exp(log_gamma[h] * (row + 1.0))
            decay_ref[h, 1] = jnp.exp(log_gamma[h] * (chunk - 1.0 - row))

    m = nb * tl
    x = x_ref[...].reshape(m, D_MODEL)
    hb = _rmsnorm(x, gpre_ref[...]).astype(BF16)
    ab = jnp.dot(hb, win_ref[:, :2 * D_CONV], preferred_element_type=F32)
    p = jnp.dot(hb, win_ref[:, 2 * D_CONV:], preferred_element_type=F32)

    u = ab[:, :D_CONV] * jax.nn.sigmoid(ab[:, D_CONV:])
    for s in range(nb):
        for lb in range(LANE_BLOCKS):
            uext_ref[s, lb, HIST_PAD:HIST_PAD + tl, :] = u[s * tl:(s + 1) * tl, _lanes(lb)]
    rb = min(CONV_ROW_BLOCK, tl)
    base = HIST_PAD - HIST
    for s in range(nb):
        for r0 in range(0, tl, rb):
            taps = []
            for lb in range(LANE_BLOCKS):
                acc = jnp.broadcast_to(cb_ref[:, _lanes(lb)], (rb, V7X_LANES))
                for j in range(CONV_WIDTH):
                    acc = acc + (cw_ref[j:j + 1, _lanes(lb)]
                                 * uext_ref[s, lb, r0 + base + j:r0 + base + j + rb, :])
                taps.append(acc)
            acc = jnp.concatenate(taps, axis=-1)
            mu = jnp.mean(acc, axis=-1, keepdims=True)
            dc = acc - mu
            var = jnp.mean(dc * dc, axis=-1, keepdims=True)
            c = _silu(dc * lax.rsqrt(var + EPS) * lng_ref[...] + lnb_ref[...])
            mix_ref[s * tl + r0:s * tl + r0 + rb, 0:D_CONV] = c.astype(BF16)
        for lb in range(LANE_BLOCKS):
            new_hist = uext_ref[s, lb, tl + base:tl + HIST_PAD, :]
            hist_ref[s, :, _lanes(lb)] = new_hist
            uext_ref[s, lb, base:HIST_PAD, :] = new_hist

    cos = cos_ref[...]
    sin = sin_ref[...]
    q0, k0, v0, g0 = (i * D_RET for i in range(4))
    for s in range(nb):
        rows = slice(s * tl, (s + 1) * tl)
        for h in range(N_RET_HEADS):
            hs = slice(h * HEAD_DIM, (h + 1) * HEAD_DIM)
            lane = lambda off: slice(off + h * HEAD_DIM, off + (h + 1) * HEAD_DIM)
            q = p[rows, lane(q0)]
            k = p[rows, lane(k0)]
            v = p[rows, lane(v0)].astype(BF16)
            gate = p[rows, lane(g0)]
            qr = (q * cos + pltpu.roll(q, HEAD_DIM // 2, 1) * sin).astype(BF16)
            kr = (k * cos + pltpu.roll(k, HEAD_DIM // 2, 1) * sin) * (HEAD_DIM ** -0.5)
            state = rstate_ref[s, h]
            g_chunk = math.exp(log_gamma[h] * chunk)
            for ci in range(tl // chunk):
                cs = slice(ci * chunk, (ci + 1) * chunk)
                qc, kc, vc = qr[cs], kr[cs], v[cs]
                sc = lax.dot_general(qc, kc.astype(BF16), (((1,), (1,)), ((), ())),
                                     preferred_element_type=F32) * dmask_ref[h]
                o = (jnp.dot(sc.astype(BF16), vc, preferred_element_type=F32)
                     + jnp.dot(qc, state.astype(BF16), preferred_element_type=F32) * decay_ref[h, 0])
                kz = (kc * decay_ref[h, 1]).astype(BF16)
                state = state * g_chunk + lax.dot_general(
                    kz, vc, (((0,), (0,)), ((), ())), preferred_element_type=F32)
                mu = jnp.mean(o, axis=-1, keepdims=True)
                do = o - mu
                var = jnp.mean(do * do, axis=-1, keepdims=True)
                on = do * lax.rsqrt(var + EPS)
                r = _silu(gate[cs]) * (on * gng_ref[:, hs])
                mix_ref[s * tl + ci * chunk:s * tl + (ci + 1) * chunk,
                        D_CONV + h * HEAD_DIM:D_CONV + (h + 1) * HEAD_DIM] = r.astype(BF16)
            rstate_ref[s, h] = state
            rout_ref[s, h] = state

    y = []
    for rows in _row_blocks(m, MIX_EPILOGUE_ROWS):
        out = jnp.dot(mix_ref[rows, :], wout_ref[...], preferred_element_type=F32)
        y.append(x[rows] + _rmsnorm(out, gpost_ref[...]))
    y_ref[...] = jnp.concatenate(y, axis=0).reshape(nb, tl, D_MODEL)


def _mixer(x, cos, sin, hist0, r0, gpre, win, cw, cb, lng, lnb, gng, wout, gpost, *, nb, tl,
           cast):
    bsz, seq, _ = x.shape
    assert bsz % nb == 0 and seq % tl == 0 and tl % V7X_SUBLANES == 0
    chunk = min(RET_CHUNK, tl)
    assert tl % chunk == 0
    m = nb * tl
    wsteps = MIX_WEIGHT_STEPS if cast else 0
    tile = lambda t: jnp.maximum(t - wsteps, 0)
    weights = (D_MODEL * D_IN + D_MODEL * D_MODEL) * 2
    scratch = (nb * (tl + HIST_PAD) * D_CONV * 4 + nb * N_RET_HEADS * HEAD_DIM * HEAD_DIM * 4
               + m * D_MODEL * 2 + N_RET_HEADS * chunk * (chunk + 2 * HEAD_DIM) * 4)
    out_specs = [
        pl.BlockSpec((nb, tl, D_MODEL), lambda b, t: (b, tile(t), 0)),
        pl.BlockSpec((nb, HIST, D_CONV), lambda b, t: (b, 0, 0)),
        pl.BlockSpec((nb, N_RET_HEADS, HEAD_DIM, HEAD_DIM), lambda b, t: (b, 0, 0, 0)),
    ]
    out_shape = [
        jax.ShapeDtypeStruct((bsz, seq, D_MODEL), F32),
        jax.ShapeDtypeStruct((bsz, HIST, D_CONV), F32),
        jax.ShapeDtypeStruct((bsz, N_RET_HEADS, HEAD_DIM, HEAD_DIM), F32),
    ]
    scratch_shapes = [
        pltpu.VMEM((nb, LANE_BLOCKS, tl + HIST_PAD, V7X_LANES), F32),
        pltpu.VMEM((nb, N_RET_HEADS, HEAD_DIM, HEAD_DIM), F32),
        pltpu.VMEM((m, D_MODEL), BF16),
        pltpu.VMEM((N_RET_HEADS, chunk, chunk), F32),
        pltpu.VMEM((N_RET_HEADS, 2, chunk, HEAD_DIM), F32),
    ]
    if cast:
        assert bsz == nb, "the weight steps run once, on the only sequence group"
        rows = D_MODEL // MIX_WEIGHT_STEPS
        row_block = lambda b, t: (jnp.minimum(t, MIX_WEIGHT_STEPS - 1), 0)
        win_spec = pl.BlockSpec((rows, D_IN), row_block)
        wout_spec = pl.BlockSpec((rows, D_MODEL), row_block)
        out_specs += [win_spec, wout_spec]
        out_shape += [jax.ShapeDtypeStruct((D_MODEL, D_IN), BF16),
                      jax.ShapeDtypeStruct((D_MODEL, D_MODEL), BF16)]
        scratch_shapes += [pltpu.VMEM((D_MODEL, D_IN), BF16), pltpu.VMEM((D_MODEL, D_MODEL), BF16)]
        weights += 2 * rows * (D_IN + D_MODEL) * 6
        body = _mixer_cast_kernel
    else:
        win_spec, wout_spec = _resident((D_MODEL, D_IN)), _resident((D_MODEL, D_MODEL))
        body = _mixer_kernel
    return pl.pallas_call(
        functools.partial(body, nb=nb, tl=tl, chunk=chunk),
        grid=(bsz // nb, wsteps + seq // tl),
        in_specs=[
            pl.BlockSpec((nb, tl, D_MODEL), lambda b, t: (b, tile(t), 0)),
            pl.BlockSpec((tl, HEAD_DIM), lambda b, t: (tile(t), 0)),
            pl.BlockSpec((tl, HEAD_DIM), lambda b, t: (tile(t), 0)),
            pl.BlockSpec((nb, HIST, D_CONV), lambda b, t: (b, 0, 0)),
            pl.BlockSpec((nb, N_RET_HEADS, HEAD_DIM, HEAD_DIM), lambda b, t: (b, 0, 0, 0)),
            _resident((1, D_MODEL)), win_spec, _resident((CONV_WIDTH, D_CONV)),
            _resident((1, D_CONV)), _resident((1, D_CONV)), _resident((1, D_CONV)),
            _resident((1, D_RET)), wout_spec, _resident((1, D_MODEL)),
        ],
        out_specs=out_specs,
        out_shape=out_shape,
        scratch_shapes=scratch_shapes,
        compiler_params=pltpu.CompilerParams(
            dimension_semantics=("arbitrary", "arbitrary"),
            vmem_limit_bytes=_vmem_limit(weights + scratch, 2 * m * D_MODEL * 4,
                                         m * (D_IN + 3 * D_MODEL) * 4)),
        name="mixer",
    )(x, cos, sin, hist0, r0, gpre, win, cw, cb, lng, lnb, gng, wout, gpost)


def _rope_tables(pos0, n):
    half = HEAD_DIM // 2
    inv = ROPE_THETA ** (-np.arange(half, dtype=np.float64) / half)
    ang = (pos0 + np.arange(n, dtype=np.float64))[:, None] * inv[None, :]
    cos, sin = np.cos(ang), np.sin(ang)
    return (np.concatenate([cos, cos], axis=-1).astype(np.float32),
            np.concatenate([-sin, sin], axis=-1).astype(np.float32))


def _ffn_both(xp, xs, ffn_w, mixer_weights=None):
    yp, ys, *converted = _ffn(xp.reshape(-1, D_MODEL), xs.reshape(-1, D_MODEL), *ffn_w,
                              mixer_weights=mixer_weights)
    return (yp.reshape(xp.shape), ys.reshape(xs.shape), *converted)


def kernel(x_prompt, x_sample, state_conv, state_ret, ffn1_norm_pre, ffn1_w_gate, ffn1_w_up, ffn1_w_down, ffn1_norm_post, mix_norm_pre, w_in, conv_w, conv_b, conv_ln_g, conv_ln_b, ret_gn_g, w_out, mix_norm_post, ffn2_norm_pre, ffn2_w_gate, ffn2_w_up, ffn2_w_down, ffn2_norm_post):
    depth = ffn1_norm_pre.shape[0]
    bp, sp, _ = x_prompt.shape
    bs, ss, _ = x_sample.shape
    row = lambda a: a.reshape(1, -1)
    yp, ys = x_prompt, x_sample
    conv_p, ret_p, conv_s, ret_s = [], [], [], []
    for l in range(depth):
        ffn1_w = (row(ffn1_norm_pre[l]), ffn1_w_gate[l], ffn1_w_up[l], ffn1_w_down[l],
                  row(ffn1_norm_post[l]))
        ffn2_w = (row(ffn2_norm_pre[l]), ffn2_w_gate[l], ffn2_w_up[l], ffn2_w_down[l],
                  row(ffn2_norm_post[l]))
        mix_w = [row(mix_norm_pre[l]), w_in[l], conv_w[l], row(conv_b[l]), row(conv_ln_g[l]),
                 row(conv_ln_b[l]), row(ret_gn_g[l]), w_out[l], row(mix_norm_post[l])]
        hist0 = jnp.zeros((bp, HIST, D_CONV), F32)
        r0 = jnp.zeros((bp, N_RET_HEADS, HEAD_DIM, HEAD_DIM), F32)
        yp, ys, mix_w[1], mix_w[7] = _ffn_both(yp, ys, ffn1_w, mixer_weights=(w_in[l], w_out[l]))
        yp, hp, rp = _mixer(yp, *_rope_tables(0, sp), hist0, r0, *mix_w,
                            nb=bp, tl=min(MIX_TILE, sp), cast=False)
        ys, hs, rs = _mixer(ys, *_rope_tables(PAST_LEN, ss), state_conv[l], state_ret[l], *mix_w,
                            nb=bs, tl=ss, cast=False)
        yp, ys = _ffn_both(yp, ys, ffn2_w)
        conv_p.append(hp); ret_p.append(rp); conv_s.append(hs); ret_s.append(rs)
    return (yp, ys, jnp.stack(conv_p), jnp.stack(ret_p), jnp.stack(conv_s), jnp.stack(ret_s))
```

```python
import functools
import math

import numpy as np

import jax
import jax.numpy as jnp
from jax import lax
from jax.experimental import pallas as pl
from jax.experimental.pallas import tpu as pltpu

D_MODEL = 1024
D_FF = 2816
D_CONV = 512
D_RET = 512
N_RET_HEADS = 4
HEAD_DIM = 128
CONV_WIDTH = 31
HIST = CONV_WIDTH - 1
ROPE_THETA = 10000.0
EPS = 1e-6
D_IN = 2 * D_CONV + 4 * D_RET
PAST_LEN = 4096

V7X_SUBLANES = 8
V7X_LANES = 128
LANE_BLOCKS = D_CONV // V7X_LANES
V7X_VMEM_BYTES = 64 * 1024 * 1024

HIST_PAD = 32
CONV_ROW_BLOCK = 32
FFN_TILE = 1024
MIX_TILE = 512
RET_CHUNK = 256
MIX_WEIGHT_STEPS = 4
MIX_EPILOGUE_ROWS = 256
FFN_EPILOGUE_ROWS = 256
V7X_MXU_DIM = 256
FF_CHUNKS = D_FF // V7X_MXU_DIM
assert FF_CHUNKS * V7X_MXU_DIM == D_FF

BF16 = jnp.bfloat16
F32 = jnp.float32


def _vmem_limit(resident_bytes, tile_bytes, temp_bytes, headroom=6 << 20):
    need = resident_bytes + 2 * tile_bytes + temp_bytes
    return int(min(need * 5 // 4 + (4 << 20), V7X_VMEM_BYTES - headroom))


def _rmsnorm(x, g):
    ms = jnp.mean(x * x, axis=-1, keepdims=True)
    return x * lax.rsqrt(ms + EPS) * g


def _lanes(block):
    return slice(block * V7X_LANES, (block + 1) * V7X_LANES)


def _silu(x):
    return x * jax.nn.sigmoid(x)


def _resident(shape):
    nd = len(shape)
    return pl.BlockSpec(shape, lambda *_: (0,) * nd, pipeline_mode=pl.Buffered(1))


def _ffn_chunk(hb, wg_ref, wu_ref, wd_ref, c):
    g = jnp.dot(hb, wg_ref[c], preferred_element_type=F32)
    u = jnp.dot(hb, wu_ref[c], preferred_element_type=F32)
    a = (_silu(g) * u).astype(BF16)
    return jnp.dot(a, wd_ref[c], preferred_element_type=F32)


def _row_blocks(rows, block):
    rb = min(block, rows)
    assert rows % rb == 0
    return [slice(r, r + rb) for r in range(0, rows, rb)]


def _ffn_math(x, gpre_ref, wg_ref, wu_ref, wd_ref, gpost_ref):
    hb = _rmsnorm(x, gpre_ref[...]).astype(BF16)
    acts = []
    for c in range(FF_CHUNKS):
        g = jnp.dot(hb, wg_ref[c], preferred_element_type=F32)
        u = jnp.dot(hb, wu_ref[c], preferred_element_type=F32)
        acts.append((_silu(g) * u).astype(BF16))
    a = jnp.concatenate(acts, axis=1)
    wd = wd_ref[...].reshape(D_FF, D_MODEL)
    out = []
    for rows in _row_blocks(x.shape[0], FFN_EPILOGUE_ROWS):
        d = jnp.dot(a[rows], wd, preferred_element_type=F32)
        out.append(x[rows] + 0.5 * _rmsnorm(d, gpost_ref[...]))
    return jnp.concatenate(out, axis=0)


def _ffn_kernel(*refs, n_tiles, hosted):
    if hosted:
        (xp_ref, xs_ref, gpre_ref, wg32_ref, wu32_ref, wd32_ref, gpost_ref, win32_ref, wout32_ref,
         yp_ref, ys_ref, win16_ref, wout16_ref, wg_ref, wu_ref, wd_ref, hb_ref) = refs
    else:
        (xp_ref, xs_ref, gpre_ref, wg32_ref, wu32_ref, wd32_ref, gpost_ref,
         yp_ref, ys_ref, wg_ref, wu_ref, wd_ref, hb_ref) = refs
    step = pl.program_id(0)
    last = FF_CHUNKS - 1
    acc_ref = yp_ref

    if hosted:
        @pl.when(step > last)
        def _():
            win16_ref[...] = win32_ref[...].astype(BF16)
            wout16_ref[...] = wout32_ref[...].astype(BF16)

    @pl.when(step <= last)
    def _():
        @pl.when(step == 0)
        def _():
            hb_ref[...] = _rmsnorm(xp_ref[...], gpre_ref[...]).astype(BF16)
            acc_ref[...] = jnp.zeros_like(acc_ref)

        wg_ref[step] = wg32_ref[...].astype(BF16)
        wu_ref[step] = wu32_ref[...].astype(BF16)
        wd_ref[step] = wd32_ref[...].astype(BF16)
        acc_ref[...] += _ffn_chunk(hb_ref[...], wg_ref, wu_ref, wd_ref, step)

        @pl.when(step == last)
        def _():
            yp_ref[...] = xp_ref[...] + 0.5 * _rmsnorm(acc_ref[...], gpost_ref[...])

    @pl.when(jnp.logical_and(step > last, step < last + n_tiles))
    def _():
        yp_ref[...] = _ffn_math(xp_ref[...], gpre_ref, wg_ref, wu_ref, wd_ref, gpost_ref)

    @pl.when(step == last + n_tiles)
    def _():
        ys_ref[...] = _ffn_math(xs_ref[...], gpre_ref, wg_ref, wu_ref, wd_ref, gpost_ref)


def _ffn(xp2d, xs2d, gpre, wg32, wu32, wd32, gpost, mixer_weights=None):
    rows, sample_rows = xp2d.shape[0], xs2d.shape[0]
    tm = FFN_TILE
    assert rows % tm == 0 and sample_rows % V7X_SUBLANES == 0
    n_tiles = rows // tm
    last = FF_CHUNKS - 1
    tile_index = lambda s: (jnp.clip(s - last, 0, n_tiles - 1), 0)
    chunk_index = lambda s: jnp.minimum(s, last)
    prompt_tile = pl.BlockSpec((tm, D_MODEL), tile_index)
    sample_tile = pl.BlockSpec((sample_rows, D_MODEL), lambda s: (0, 0))
    sample_in = pl.BlockSpec((sample_rows, D_MODEL), lambda s: (0, 0), pipeline_mode=pl.Buffered(1))
    w_bytes = 3 * D_MODEL * D_FF
    hosted = mixer_weights is not None
    extra_in, extra_out, extra_shape, extra_args, extra_bytes = [], [], [], [], 0
    if hosted:
        assert D_MODEL % n_tiles == 0
        hrows = D_MODEL // n_tiles
        hblock = lambda s: (jnp.clip(s - FF_CHUNKS, 0, n_tiles - 1), 0)
        extra_in = [pl.BlockSpec((hrows, D_IN), hblock), pl.BlockSpec((hrows, D_MODEL), hblock)]
        extra_out = list(extra_in)
        extra_shape = [jax.ShapeDtypeStruct((D_MODEL, D_IN), BF16),
                       jax.ShapeDtypeStruct((D_MODEL, D_MODEL), BF16)]
        extra_args = list(mixer_weights)
        extra_bytes = 2 * hrows * (D_IN + D_MODEL) * 6
    return pl.pallas_call(
        functools.partial(_ffn_kernel, n_tiles=n_tiles, hosted=hosted),
        grid=(last + n_tiles + 1,),
        in_specs=[prompt_tile, sample_in, _resident((1, D_MODEL)),
                  pl.BlockSpec((D_MODEL, V7X_MXU_DIM), lambda s: (0, chunk_index(s))),
                  pl.BlockSpec((D_MODEL, V7X_MXU_DIM), lambda s: (0, chunk_index(s))),
                  pl.BlockSpec((V7X_MXU_DIM, D_MODEL), lambda s: (chunk_index(s), 0)),
                  _resident((1, D_MODEL))] + extra_in,
        out_specs=[prompt_tile, sample_tile] + extra_out,
        out_shape=[jax.ShapeDtypeStruct((rows, D_MODEL), F32),
                   jax.ShapeDtypeStruct((sample_rows, D_MODEL), F32)] + extra_shape,
        scratch_shapes=[pltpu.VMEM((FF_CHUNKS, D_MODEL, V7X_MXU_DIM), BF16),
                        pltpu.VMEM((FF_CHUNKS, D_MODEL, V7X_MXU_DIM), BF16),
                        pltpu.VMEM((FF_CHUNKS, V7X_MXU_DIM, D_MODEL), BF16),
                        pltpu.VMEM((tm, D_MODEL), BF16)],
        compiler_params=pltpu.CompilerParams(
            dimension_semantics=("arbitrary",),
            vmem_limit_bytes=_vmem_limit(
                2 * w_bytes + 2 * 3 * D_MODEL * V7X_MXU_DIM * 4 + tm * D_MODEL * 6 + extra_bytes,
                2 * (tm + sample_rows) * D_MODEL * 4,
                tm * (3 * V7X_MXU_DIM + 3 * D_MODEL) * 4,
                headroom=(1 << 20) if hosted else (6 << 20))),
        name="ffn",
    )(xp2d, xs2d, gpre, wg32, wu32, wd32, gpost, *extra_args)


def _cast_rows(step, src_ref, out_ref, resident_ref):
    rows = src_ref.shape[0]
    block = src_ref[...].astype(BF16)
    out_ref[...] = block
    resident_ref[pl.ds(pl.multiple_of(step * rows, rows), rows), :] = block


def _mixer_kernel(*refs, nb, tl, chunk):
    _mixer_tile(pl.program_id(1), *refs, nb=nb, tl=tl, chunk=chunk)


def _mixer_cast_kernel(x_ref, cos_ref, sin_ref, hist0_ref, r0_ref, gpre_ref, win32_ref, cw_ref,
                       cb_ref, lng_ref, lnb_ref, gng_ref, wout32_ref, gpost_ref,
                       y_ref, hist_ref, rout_ref, win16_ref, wout16_ref,
                       uext_ref, rstate_ref, mix_ref, dmask_ref, decay_ref, win_ref, wout_ref, *,
                       nb, tl, chunk):
    step = pl.program_id(1)

    @pl.when(step < MIX_WEIGHT_STEPS)
    def _():
        _cast_rows(step, win32_ref, win16_ref, win_ref)
        _cast_rows(step, wout32_ref, wout16_ref, wout_ref)

    @pl.when(step >= MIX_WEIGHT_STEPS)
    def _():
        _mixer_tile(step - MIX_WEIGHT_STEPS, x_ref, cos_ref, sin_ref, hist0_ref, r0_ref, gpre_ref,
                    win_ref, cw_ref, cb_ref, lng_ref, lnb_ref, gng_ref, wout_ref, gpost_ref,
                    y_ref, hist_ref, rout_ref, uext_ref, rstate_ref, mix_ref, dmask_ref,
                    decay_ref, nb=nb, tl=tl, chunk=chunk)


def _mixer_tile(t, x_ref, cos_ref, sin_ref, hist0_ref, r0_ref, gpre_ref, win_ref, cw_ref,
                cb_ref, lng_ref, lnb_ref, gng_ref, wout_ref, gpost_ref,
                y_ref, hist_ref, rout_ref,
                uext_ref, rstate_ref, mix_ref, dmask_ref, decay_ref, *, nb, tl, chunk):
    log_gamma = [math.log1p(-(2.0 ** (-5.0 - h))) for h in range(N_RET_HEADS)]

    @pl.when(t == 0)
    def _init():
        for lb in range(LANE_BLOCKS):
            uext_ref[:, lb, HIST_PAD - HIST:HIST_PAD, :] = hist0_ref[:, :, _lanes(lb)]
        rstate_ref[...] = r0_ref[...]
        ii = lax.broadcasted_iota(jnp.int32, (chunk, chunk), 0)
        jj = lax.broadcasted_iota(jnp.int32, (chunk, chunk), 1)
        rel = (ii - jj).astype(F32)
        row = lax.broadcasted_iota(jnp.int32, (chunk, HEAD_DIM), 0).astype(F32)
        for h in range(N_RET_HEADS):
            dmask_ref[h] = jnp.where(rel >= 0.0, jnp.exp(log_gamma[h] * jnp.maximum(rel, 0.0)), 0.0)
            decay_ref[h, 0] = jnp.exp(log_gamma[h] * (row + 1.0))
            decay_ref[h, 1] = jnp.exp(log_gamma[h] * (chunk - 1.0 - row))

    m = nb * tl
    x = x_ref[...].reshape(m, D_MODEL)
    hb = _rmsnorm(x, gpre_ref[...]).astype(BF16)
    ab = jnp.dot(hb, win_ref[:, :2 * D_CONV], preferred_element_type=F32)
    p = jnp.dot(hb, win_ref[:, 2 * D_CONV:], preferred_element_type=F32)

    u = ab[:, :D_CONV] * jax.nn.sigmoid(ab[:, D_CONV:])
    for s in range(nb):
        for lb in range(LANE_BLOCKS):
            uext_ref[s, lb, HIST_PAD:HIST_PAD + tl, :] = u[s * tl:(s + 1) * tl, _lanes(lb)]
    rb = min(CONV_ROW_BLOCK, tl)
    base = HIST_PAD - HIST
    for s in range(nb):
        for r0 in range(0, tl, rb):
            taps = []
            for lb in range(LANE_BLOCKS):
                acc = jnp.broadcast_to(cb_ref[:, _lanes(lb)], (rb, V7X_LANES))
                for j in range(CONV_WIDTH):
                    acc = acc + (cw_ref[j:j + 1, _lanes(lb)]
                                 * uext_ref[s, lb, r0 + base + j:r0 + base + j + rb, :])
                taps.append(acc)
            acc = jnp.concatenate(taps, axis=-1)
            mu = jnp.mean(acc, axis=-1, keepdims=True)
            dc = acc - mu
            var = jnp.mean(dc * dc, axis=-1, keepdims=True)
            c = _silu(dc * lax.rsqrt(var + EPS) * lng_ref[...] + lnb_ref[...])
            mix_ref[s * tl + r0:s * tl + r0 + rb, 0:D_CONV] = c.astype(BF16)
        for lb in range(LANE_BLOCKS):
            new_hist = uext_ref[s, lb, tl + base:tl + HIST_PAD, :]
            hist_ref[s, :, _lanes(lb)] = new_hist
            uext_ref[s, lb, base:HIST_PAD, :] = new_hist

    cos = cos_ref[...]
    sin = sin_ref[...]
    q0, k0, v0, g0 = (i * D_RET for i in range(4))
    for s in range(nb):
        rows = slice(s * tl, (s + 1) * tl)
        for h in range(N_RET_HEADS):
            hs = slice(h * HEAD_DIM, (h + 1) * HEAD_DIM)
            lane = lambda off: slice(off + h * HEAD_DIM, off + (h + 1) * HEAD_DIM)
            q = p[rows, lane(q0)]
            k = p[rows, lane(k0)]
            v = p[rows, lane(v0)].astype(BF16)
            gate = p[rows, lane(g0)]
            qr = (q * cos + pltpu.roll(q, HEAD_DIM // 2, 1) * sin).astype(BF16)
            kr = (k * cos + pltpu.roll(k, HEAD_DIM // 2, 1) * sin) * (HEAD_DIM ** -0.5)
            state = rstate_ref[s, h]
            g_chunk = math.exp(log_gamma[h] * chunk)
            for ci in range(tl // chunk):
                cs = slice(ci * chunk, (ci + 1) * chunk)
                qc, kc, vc = qr[cs], kr[cs], v[cs]
                sc = lax.dot_general(qc, kc.astype(BF16), (((1,), (1,)), ((), ())),
                                     preferred_element_type=F32) * dmask_ref[h]
                o = (jnp.dot(sc.astype(BF16), vc, preferred_element_type=F32)
                     + jnp.dot(qc, state.astype(BF16), preferred_element_type=F32) * decay_ref[h, 0])
                kz = (kc * decay_ref[h, 1]).astype(BF16)
                state = state * g_chunk + lax.dot_general(
                    kz, vc, (((0,), (0,)), ((), ())), preferred_element_type=F32)
                mu = jnp.mean(o, axis=-1, keepdims=True)
                do = o - mu
                var = jnp.mean(do * do, axis=-1, keepdims=True)
                on = do * lax.rsqrt(var + EPS)
                r = _silu(gate[cs]) * (on * gng_ref[:, hs])
                mix_ref[s * tl + ci * chunk:s * tl + (ci + 1) * chunk,
                        D_CONV + h * HEAD_DIM:D_CONV + (h + 1) * HEAD_DIM] = r.astype(BF16)
            rstate_ref[s, h] = state
            rout_ref[s, h] = state

    y = []
    for rows in _row_blocks(m, MIX_EPILOGUE_ROWS):
        out = jnp.dot(mix_ref[rows, :], wout_ref[...], preferred_element_type=F32)
        y.append(x[rows] + _rmsnorm(out, gpost_ref[...]))
    y_ref[...] = jnp.concatenate(y, axis=0).reshape(nb, tl, D_MODEL)


def _mixer(x, cos, sin, hist0, r0, gpre, win, cw, cb, lng, lnb, gng, wout, gpost, *, nb, tl,
           cast):
    bsz, seq, _ = x.shape
    assert bsz % nb == 0 and seq % tl == 0 and tl % V7X_SUBLANES == 0
    chunk = min(RET_CHUNK, tl)
    assert tl % chunk == 0
    m = nb * tl
    wsteps = MIX_WEIGHT_STEPS if cast else 0
    tile = lambda t: jnp.maximum(t - wsteps, 0)
    weights = (D_MODEL * D_IN + D_MODEL * D_MODEL) * 2
    scratch = (nb * (tl + HIST_PAD) * D_CONV * 4 + nb * N_RET_HEADS * HEAD_DIM * HEAD_DIM * 4
               + m * D_MODEL * 2 + N_RET_HEADS * chunk * (chunk + 2 * HEAD_DIM) * 4)
    out_specs = [
        pl.BlockSpec((nb, tl, D_MODEL), lambda b, t: (b, tile(t), 0)),
        pl.BlockSpec((nb, HIST, D_CONV), lambda b, t: (b, 0, 0)),
        pl.BlockSpec((nb, N_RET_HEADS, HEAD_DIM, HEAD_DIM), lambda b, t: (b, 0, 0, 0)),
    ]
    out_shape = [
        jax.ShapeDtypeStruct((bsz, seq, D_MODEL), F32),
        jax.ShapeDtypeStruct((bsz, HIST, D_CONV), F32),
        jax.ShapeDtypeStruct((bsz, N_RET_HEADS, HEAD_DIM, HEAD_DIM), F32),
    ]
    scratch_shapes = [
        pltpu.VMEM((nb, LANE_BLOCKS, tl + HIST_PAD, V7X_LANES), F32),
        pltpu.VMEM((nb, N_RET_HEADS, HEAD_DIM, HEAD_DIM), F32),
        pltpu.VMEM((m, D_MODEL), BF16),
        pltpu.VMEM((N_RET_HEADS, chunk, chunk), F32),
        pltpu.VMEM((N_RET_HEADS, 2, chunk, HEAD_DIM), F32),
    ]
    if cast:
        assert bsz == nb, "the weight steps run once, on the only sequence group"
        rows = D_MODEL // MIX_WEIGHT_STEPS
        row_block = lambda b, t: (jnp.minimum(t, MIX_WEIGHT_STEPS - 1), 0)
        win_spec = pl.BlockSpec((rows, D_IN), row_block)
        wout_spec = pl.BlockSpec((rows, D_MODEL), row_block)
        out_specs += [win_spec, wout_spec]
        out_shape += [jax.ShapeDtypeStruct((D_MODEL, D_IN), BF16),
                      jax.ShapeDtypeStruct((D_MODEL, D_MODEL), BF16)]
        scratch_shapes += [pltpu.VMEM((D_MODEL, D_IN), BF16), pltpu.VMEM((D_MODEL, D_MODEL), BF16)]
        weights += 2 * rows * (D_IN + D_MODEL) * 6
        body = _mixer_cast_kernel
    else:
        win_spec, wout_spec = _resident((D_MODEL, D_IN)), _resident((D_MODEL, D_MODEL))
        body = _mixer_kernel
    return pl.pallas_call(
        functools.partial(body, nb=nb, tl=tl, chunk=chunk),
        grid=(bsz // nb, wsteps + seq // tl),
        in_specs=[
            pl.BlockSpec((nb, tl, D_MODEL), lambda b, t: (b, tile(t), 0)),
            pl.BlockSpec((tl, HEAD_DIM), lambda b, t: (tile(t), 0)),
            pl.BlockSpec((tl, HEAD_DIM), lambda b, t: (tile(t), 0)),
            pl.BlockSpec((nb, HIST, D_CONV), lambda b, t: (b, 0, 0)),
            pl.BlockSpec((nb, N_RET_HEADS, HEAD_DIM, HEAD_DIM), lambda b, t: (b, 0, 0, 0)),
            _resident((1, D_MODEL)), win_spec, _resident((CONV_WIDTH, D_CONV)),
            _resident((1, D_CONV)), _resident((1, D_CONV)), _resident((1, D_CONV)),
            _resident((1, D_RET)), wout_spec, _resident((1, D_MODEL)),
        ],
        out_specs=out_specs,
        out_shape=out_shape,
        scratch_shapes=scratch_shapes,
        compiler_params=pltpu.CompilerParams(
            dimension_semantics=("arbitrary", "arbitrary"),
            vmem_limit_bytes=_vmem_limit(weights + scratch, 2 * m * D_MODEL * 4,
                                         m * (D_IN + 3 * D_MODEL) * 4)),
        name="mixer",
    )(x, cos, sin, hist0, r0, gpre, win, cw, cb, lng, lnb, gng, wout, gpost)


def _rope_tables(pos0, n):
    half = HEAD_DIM // 2
    inv = ROPE_THETA ** (-np.arange(half, dtype=np.float64) / half)
    ang = (pos0 + np.arange(n, dtype=np.float64))[:, None] * inv[None, :]
    cos, sin = np.cos(ang), np.sin(ang)
    return (np.concatenate([cos, cos], axis=-1).astype(np.float32),
            np.concatenate([-sin, sin], axis=-1).astype(np.float32))


def _ffn_both(xp, xs, ffn_w, mixer_weights=None):
    yp, ys, *converted = _ffn(xp.reshape(-1, D_MODEL), xs.reshape(-1, D_MODEL), *ffn_w,
                              mixer_weights=mixer_weights)
    return (yp.reshape(xp.shape), ys.reshape(xs.shape), *converted)


def kernel(x_prompt, x_sample, state_conv, state_ret, ffn1_norm_pre, ffn1_w_gate, ffn1_w_up, ffn1_w_down, ffn1_norm_post, mix_norm_pre, w_in, conv_w, conv_b, conv_ln_g, conv_ln_b, ret_gn_g, w_out, mix_norm_post, ffn2_norm_pre, ffn2_w_gate, ffn2_w_up, ffn2_w_down, ffn2_norm_post):
    depth = ffn1_norm_pre.shape[0]
    bp, sp, _ = x_prompt.shape
    bs, ss, _ = x_sample.shape
    row = lambda a: a.reshape(1, -1)
    yp, ys = x_prompt, x_sample
    conv_p, ret_p, conv_s, ret_s = [], [], [], []
    for l in range(depth):
        ffn1_w = (row(ffn1_norm_pre[l]), ffn1_w_gate[l], ffn1_w_up[l], ffn1_w_down[l],
                  row(ffn1_norm_post[l]))
        ffn2_w = (row(ffn2_norm_pre[l]), ffn2_w_gate[l], ffn2_w_up[l], ffn2_w_down[l],
                  row(ffn2_norm_post[l]))
        mix_w = [row(mix_norm_pre[l]), w_in[l], conv_w[l], row(conv_b[l]), row(conv_ln_g[l]),
                 row(conv_ln_b[l]), row(ret_gn_g[l]), w_out[l], row(mix_norm_post[l])]
        hist0 = jnp.zeros((bp, HIST, D_CONV), F32)
        r0 = jnp.zeros((bp, N_RET_HEADS, HEAD_DIM, HEAD_DIM), F32)
        yp, ys, mix_w[1], mix_w[7] = _ffn_both(yp, ys, ffn1_w, mixer_weights=(w_in[l], w_out[l]))
        yp, hp, rp = _mixer(yp, *_rope_tables(0, sp), hist0, r0, *mix_w,
                            nb=bp, tl=min(MIX_TILE, sp), cast=False)
        ys, hs, rs = _mixer(ys, *_rope_tables(PAST_LEN, ss), state_conv[l], state_ret[l], *mix_w,
                            nb=bs, tl=ss, cast=False)
        yp, ys = _ffn_both(yp, ys, ffn2_w)
        conv_p.append(hp); ret_p.append(rp); conv_s.append(hs); ret_s.append(rs)
    return (yp, ys, jnp.stack(conv_p), jnp.stack(ret_p), jnp.stack(conv_s), jnp.stack(ret_s))
```
